```python
import math
import functools
import jax
import jax.numpy as jnp
from jax import lax
import numpy as np

D_MODEL = 1024
BATCH = 1
SEQ = 16384
DEPTH = 2
DEC_BATCH = 32
DEC_SEQ = 8
PAST_LEN = 16384
PAGE_SIZE = 128

HA = 4
DK = 64
DV = 2 * DK
A_WIDTH = HA * DV
S5_GROUP = 16
S5_STATE = 64
S5_WIDTH = 512
S5_GROUPS = S5_WIDTH // S5_GROUP
HC = 8
NC = 64
C_WIDTH = HC * NC
DECAY_LORA = 64
AAA_LORA = 64
GATE_LORA = 128
C_PROJ = 3 * C_WIDTH + DECAY_LORA + AAA_LORA + GATE_LORA
N_MEM = 256
HX = 4
DX = 64
X_WIDTH = HX * DX
D_FF = -(-(8 * D_MODEL) // (3 * 256)) * 256
QA_COLS = HA * 2 * DK
KA_COLS = HA * 2 * DK
VA_COLS = HA * DV
GATE_COLS = 3 * D_MODEL
IN_COLS = QA_COLS + KA_COLS + VA_COLS + S5_WIDTH + C_PROJ + GATE_COLS
IN_SPLITS = (QA_COLS, QA_COLS + KA_COLS, QA_COLS + KA_COLS + VA_COLS,
             QA_COLS + KA_COLS + VA_COLS + S5_WIDTH,
             QA_COLS + KA_COLS + VA_COLS + S5_WIDTH + C_PROJ)
C_SPLITS = (C_WIDTH, 2 * C_WIDTH, 3 * C_WIDTH, 3 * C_WIDTH + DECAY_LORA,
            3 * C_WIDTH + DECAY_LORA + AAA_LORA)
BLOCK_Q = 128
RMS_EPS = 1e-6
GN_EPS = 64e-5
NEG_INF = -1e30

kernel_name = 'hybrid_diffattn_s5_rwkv7_decoder_step'

F32 = jnp.float32


def rmsnorm(x, g, eps=RMS_EPS):
    xf = x.astype(F32)
    y = xf * lax.rsqrt(jnp.mean(xf * xf, axis=-1, keepdims=True) + eps)
    return (y * g.astype(F32)).astype(x.dtype)


def diff_attend(q, segments, lam):
    scale = DK ** -0.5
    scores = []
    for k, v, mask in segments:
        s = jnp.einsum('bqhcd,bkhcd->bhcqk', q, k).astype(F32) * scale
        scores.append(jnp.where(mask, s, NEG_INF))
    p = jax.nn.softmax(jnp.concatenate(scores, axis=-1), axis=-1)
    p = p[:, :, 0] - lam * p[:, :, 1]
    out = 0.0
    start = 0
    for k, v, mask in segments:
        n = k.shape[1]
        out = out + jnp.einsum('bhqk,bkhd->bqhd', p[..., start:start + n].astype(v.dtype), v)
        start += n
    return out


def diff_attn_prompt(q, k, v, lam):
    B, L = q.shape[0], q.shape[1]
    nb = L // BLOCK_Q
    qb = jnp.moveaxis(q.reshape(B, nb, BLOCK_Q, HA, 2, DK), 1, 0)
    k_pos = jnp.arange(L)

    def block(args):
        qi, i = args
        q_pos = i * BLOCK_Q + jnp.arange(BLOCK_Q)
        mask = k_pos[None, :] <= q_pos[:, None]
        return diff_attend(qi, ((k, v, mask),), lam)

    out = lax.map(block, (qb, jnp.arange(nb)))
    return jnp.moveaxis(out, 0, 1).reshape(B, L, HA, DV)


def diff_attn_sample(cache_k_l, cache_v_l, page_table, q, k, v, lam):
    Bd, T = q.shape[0], q.shape[1]
    past = page_table.shape[1] * PAGE_SIZE
    k_past = cache_k_l[page_table].reshape(Bd, past, HA, 2, DK)
    v_past = cache_v_l[page_table].reshape(Bd, past, HA, DV)
    mask_past = jnp.ones((T, past), dtype=bool)
    mask_new = jnp.tril(jnp.ones((T, T), dtype=bool))
    return diff_attend(q, ((k_past, v_past, mask_past), (k, v, mask_new)), lam)


def s5_discretize(lam_re, lam_im, log_dt, b_re, b_im):
    lam_re, lam_im = lam_re.astype(F32), lam_im.astype(F32)
    b_re, b_im = b_re.astype(F32), b_im.astype(F32)
    dt = jnp.exp(log_dt.astype(F32))[:, None]
    mag = jnp.exp(lam_re * dt)
    ab_re = mag * jnp.cos(lam_im * dt)
    ab_im = mag * jnp.sin(lam_im * dt)
    den = lam_re * lam_re + lam_im * lam_im
    nr = ab_re - 1.0
    cr = (nr * lam_re + ab_im * lam_im) / den
    ci = (ab_im * lam_re - nr * lam_im) / den
    bb_re = cr[..., None] * b_re - ci[..., None] * b_im
    bb_im = cr[..., None] * b_im + ci[..., None] * b_re
    return ab_re, ab_im, bb_re, bb_im


def s5_combine(e1, e2):
    a1r, a1i, b1r, b1i = e1
    a2r, a2i, b2r, b2i = e2
    return (a2r * a1r - a2i * a1i, a2r * a1i + a2i * a1r,
            a2r * b1r - a2i * b1i + b2r, a2r * b1i + a2i * b1r + b2i)


def s5_branch(u, x0_re, x0_im, lp):
    B, L, _ = u.shape
    uf = u.astype(F32)
    ab_re, ab_im, bb_re, bb_im = s5_discretize(lp['s5_lam_re'], lp['s5_lam_im'], lp['s5_log_dt'],
                                               lp['s5_b_re'], lp['s5_b_im'])
    ug = uf.reshape(B, L, S5_GROUPS, S5_GROUP)
    bu_re = jnp.einsum('blgh,gph->blgp', ug, bb_re)
    bu_im = jnp.einsum('blgh,gph->blgp', ug, bb_im)
    x0r, x0i = x0_re.astype(F32), x0_im.astype(F32)
    bu_re = bu_re.at[:, 0].add(ab_re * x0r - ab_im * x0i)
    bu_im = bu_im.at[:, 0].add(ab_re * x0i + ab_im * x0r)
    a_re = jnp.broadcast_to(ab_re, bu_re.shape)
    a_im = jnp.broadcast_to(ab_im, bu_im.shape)
    _, _, xr, xi = lax.associative_scan(s5_combine, (a_re, a_im, bu_re, bu_im), axis=1)
    y = (jnp.einsum('blgp,ghp->blgh', xr, lp['s5_c_re'].astype(F32))
         - jnp.einsum('blgp,ghp->blgh', xi, lp['s5_c_im'].astype(F32)))
    y = y.reshape(B, L, S5_WIDTH) + lp['s5_d'].astype(F32) * uf
    z = jax.nn.gelu(y)
    out = z * jax.nn.sigmoid(z @ lp['s5_w_glu'].astype(F32) + lp['s5_b_glu'].astype(F32))
    return out.astype(u.dtype), xr[:, -1].astype(x0_re.dtype), xi[:, -1].astype(x0_im.dtype)


def rwkv_branch(pc, shift_prev, s0, lp):
    B, L, _ = pc.shape
    shifted = jnp.concatenate([shift_prev[:, None, :].astype(pc.dtype), pc[:, :-1]], axis=1)
    xm = pc + (shifted - pc) * lp['rw_mu']
    r, k, v, pw, pa, pg = jnp.split(xm, C_SPLITS, axis=-1)
    w_log = -jax.nn.softplus(-(lp['rw_w0'] + jnp.tanh(pw) @ lp['rw_w2']).astype(F32)) - 0.5
    decay = jnp.exp(-jnp.exp(w_log))
    a = jax.nn.sigmoid((lp['rw_a0'] + pa @ lp['rw_a2']).astype(F32))
    g = (jax.nn.sigmoid(pg) @ lp['rw_g2']).astype(F32)

    def heads(t):
        return t.astype(F32).reshape(B, L, HC, NC)

    r_h, k_h, v_h, a_h, w_h = heads(r), heads(k), heads(v), heads(a), heads(decay)
    kk = k_h * lp['rw_k_k'].astype(F32).reshape(HC, NC)
    kk = kk / jnp.maximum(jnp.linalg.norm(kk, axis=-1, keepdims=True), 1e-12)
    k_h = k_h * (1.0 + (a_h - 1.0) * lp['rw_k_a'].astype(F32).reshape(HC, NC))

    def step(S, inp):
        r_t, w_t, k_t, v_t, kk_t, a_t = inp
        sa = jnp.einsum('bhvk,bhk->bhv', S, -kk_t)
        S = (S * w_t[:, :, None, :] + sa[..., None] * (kk_t * a_t)[:, :, None, :]
             + v_t[..., None] * k_t[:, :, None, :])
        return S, jnp.einsum('bhvk,bhk->bhv', S, r_t)

    xs = tuple(jnp.moveaxis(t, 1, 0) for t in (r_h, w_h, k_h, v_h, kk, a_h))
    s_final, ys = lax.scan(step, s0.astype(F32), xs)
    y = jnp.moveaxis(ys, 0, 1)
    mu = jnp.mean(y, axis=-1, keepdims=True)
    var = jnp.mean(jnp.square(y - mu), axis=-1, keepdims=True)
    y = ((y - mu) * lax.rsqrt(var + GN_EPS)).reshape(B, L, C_WIDTH)
    y = y * lp['rw_ln_w'].astype(F32) + lp['rw_ln_b'].astype(F32)
    bonus = jnp.sum(r_h * k_h * lp['rw_r_k'].astype(F32), axis=-1, keepdims=True) * v_h
    y = (y + bonus.reshape(B, L, C_WIDTH)) * g
    return y.astype(pc.dtype), s_final.astype(s0.dtype), pc[:, -1]


def memory_kv(mem, lp):
    B = mem.shape[0]
    m = rmsnorm(mem, lp['norm_mem'])
    return ((m @ lp['w_mk']).reshape(B, N_MEM, HX, DX), (m @ lp['w_mv']).reshape(B, N_MEM, HX, DX))


def cross_attend(h, mem_k, mem_v, lp):
    B, L, _ = h.shape
    q = (h @ lp['w_xq']).reshape(B, L, HX, DX)
    s = jnp.einsum('bqhd,bmhd->bhqm', q, mem_k).astype(F32) * (DX ** -0.5)
    p = jax.nn.softmax(s, axis=-1)
    o = jnp.einsum('bhqm,bmhd->bqhd', p.astype(mem_v.dtype), mem_v).reshape(B, L, X_WIDTH)
    return o @ lp['w_xo']


def trunk_layer(x, attn_fn, mem_k, mem_v, ssm_re0, ssm_im0, rwkv0, shift0, lp, lam_init):
    B, L, _ = x.shape
    h = rmsnorm(x, lp['norm_mix'])
    p = h @ lp['w_in']
    q, k, v, u, pc, gates = jnp.split(p, IN_SPLITS, axis=-1)
    q = q.reshape(B, L, HA, 2, DK)
    k = k.reshape(B, L, HA, 2, DK)
    v = v.reshape(B, L, HA, DV)
    lam = (jnp.exp(jnp.sum(lp['lam_q1'].astype(F32) * lp['lam_k1'].astype(F32)))
           - jnp.exp(jnp.sum(lp['lam_q2'].astype(F32) * lp['lam_k2'].astype(F32))) + lam_init)
    ya = attn_fn(q, k, v, lam)
    ya = (rmsnorm(ya, lp['subln']) * (1.0 - lam_init)).reshape(B, L, A_WIDTH)
    yb, ssm_re, ssm_im = s5_branch(u, ssm_re0, ssm_im0, lp)
    yc, rwkv_s, shift = rwkv_branch(pc, shift0, rwkv0, lp)
    g_a, g_b, g_c = jnp.split(jax.nn.sigmoid(gates), 3, axis=-1)
    merged = g_a * (ya @ lp['proj_a']) + g_b * (yb @ lp['proj_b']) + g_c * (yc @ lp['proj_c'])
    x = x + merged @ lp['w_out']
    x = x + cross_attend(rmsnorm(x, lp['norm_x']), mem_k, mem_v, lp)
    hf = rmsnorm(x, lp['norm_ffn'])
    x = x + (jax.nn.silu(hf @ lp['w_gate']) * (hf @ lp['w_up'])) @ lp['w_down']
    return x, k, v, ssm_re, ssm_im, rwkv_s, shift


def setup_inputs(seed: int = 0) -> dict:
    key = jax.random.key(seed)
    ks = iter(jax.random.split(key, 80))

    def nrm(shape, scale):
        return jax.random.normal(next(ks), shape, F32) * scale

    def gain(shape):
        return 1.0 + nrm(shape, 0.02)

    def unif(shape, lo, hi):
        return jax.random.uniform(next(ks), shape, F32, minval=lo, maxval=hi)

    n_pages = PAST_LEN // PAGE_SIZE
    n_used = DEC_BATCH * n_pages
    n_pool = n_used + n_used // 4 + 1
    page_table = jax.random.permutation(next(ks), n_pool)[:n_used].reshape(DEC_BATCH, n_pages).astype(jnp.int32)
    n_idx = jnp.arange(S5_STATE, dtype=F32)
    D = D_MODEL
    return {
        'x_prompt': nrm((BATCH, SEQ, D), 1.0),
        'x_sample': nrm((DEC_BATCH, DEC_SEQ, D), 1.0),
        'cache_k': nrm((DEPTH, n_pool, PAGE_SIZE, HA, 2, DK), 1.0),
        'cache_v': nrm((DEPTH, n_pool, PAGE_SIZE, HA, DV), 1.0),
        'cache_mem_k': nrm((DEPTH, DEC_BATCH, N_MEM, HX, DX), 1.0),
        'cache_mem_v': nrm((DEPTH, DEC_BATCH, N_MEM, HX, DX), 1.0),
        'state_ssm_re': nrm((DEPTH, DEC_BATCH, S5_GROUPS, S5_STATE), 0.1),
        'state_ssm_im': nrm((DEPTH, DEC_BATCH, S5_GROUPS, S5_STATE), 0.1),
        'state_rwkv': nrm((DEPTH, DEC_BATCH, HC, NC, NC), 0.3),
        'state_shift': nrm((DEPTH, DEC_BATCH, C_PROJ), 1.0),
        'page_table': page_table,
        'mem_prompt': nrm((BATCH, N_MEM, D), 1.0),
        'norm_mix': gain((DEPTH, D)),
        'w_in': nrm((DEPTH, D, IN_COLS), D ** -0.5),
        'lam_q1': nrm((DEPTH, DK), 0.1),
        'lam_k1': nrm((DEPTH, DK), 0.1),
        'lam_q2': nrm((DEPTH, DK), 0.1),
        'lam_k2': nrm((DEPTH, DK), 0.1),
        'subln': gain((DEPTH, DV)),
        'proj_a': nrm((DEPTH, A_WIDTH, D), A_WIDTH ** -0.5),
        's5_lam_re': -0.5 + nrm((DEPTH, S5_GROUPS, S5_STATE), 0.01),
        's5_lam_im': math.pi * n_idx + nrm((DEPTH, S5_GROUPS, S5_STATE), 0.01),
        's5_log_dt': unif((DEPTH, S5_GROUPS), math.log(1e-3), math.log(1e-1)),
        's5_b_re': nrm((DEPTH, S5_GROUPS, S5_STATE, S5_GROUP), S5_GROUP ** -0.5),
        's5_b_im': nrm((DEPTH, S5_GROUPS, S5_STATE, S5_GROUP), S5_GROUP ** -0.5),
        's5_c_re': nrm((DEPTH, S5_GROUPS, S5_GROUP, S5_STATE), 0.5),
        's5_c_im': nrm((DEPTH, S5_GROUPS, S5_GROUP, S5_STATE), 0.5),
        's5_d': nrm((DEPTH, S5_WIDTH), 1.0),
        's5_w_glu': nrm((DEPTH, S5_WIDTH, S5_WIDTH), S5_WIDTH ** -0.5),
        's5_b_glu': nrm((DEPTH, S5_WIDTH), 0.01),
        'proj_b': nrm((DEPTH, S5_WIDTH, D), S5_WIDTH ** -0.5),
        'rw_mu': unif((DEPTH, C_PROJ), 0.0, 1.0),
        'rw_w0': unif((DEPTH, C_WIDTH), -6.0, 0.5),
        'rw_w2': nrm((DEPTH, DECAY_LORA, C_WIDTH), 0.1 * DECAY_LORA ** -0.5),
        'rw_a0': nrm((DEPTH, C_WIDTH), 0.1),
        'rw_a2': nrm((DEPTH, AAA_LORA, C_WIDTH), 0.1 * AAA_LORA ** -0.5),
        'rw_g2': nrm((DEPTH, GATE_LORA, C_WIDTH), GATE_LORA ** -0.5),
        'rw_k_k': 0.85 + nrm((DEPTH, C_WIDTH), 0.02),
        'rw_k_a': 1.0 + nrm((DEPTH, C_WIDTH), 0.02),
        'rw_r_k': nrm((DEPTH, HC, NC), 0.1),
        'rw_ln_w': gain((DEPTH, C_WIDTH)),
        'rw_ln_b': nrm((DEPTH, C_WIDTH), 0.01),
        'proj_c': nrm((DEPTH, C_WIDTH, D), C_WIDTH ** -0.5),
        'w_out': nrm((DEPTH, D, D), D ** -0.5),
        'norm_x': gain((DEPTH, D)),
        'norm_mem': gain((DEPTH, D)),
        'w_xq': nrm((DEPTH, D, X_WIDTH), D ** -0.5),
        'w_mk': nrm((DEPTH, D, X_WIDTH), D ** -0.5),
        'w_mv': nrm((DEPTH, D, X_WIDTH), D ** -0.5),
        'w_xo': nrm((DEPTH, X_WIDTH, D), X_WIDTH ** -0.5),
        'norm_ffn': gain((DEPTH, D)),
        'w_gate': nrm((DEPTH, D, D_FF), D ** -0.5),
        'w_up': nrm((DEPTH, D, D_FF), D ** -0.5),
        'w_down': nrm((DEPTH, D_FF, D), D_FF ** -0.5),
        'norm_final': gain((D,)),
    }


def reference(x_prompt, x_sample, cache_k, cache_v, cache_mem_k, cache_mem_v,
              state_ssm_re, state_ssm_im, state_rwkv, state_shift, page_table, mem_prompt,
              norm_mix, w_in, lam_q1, lam_k1, lam_q2, lam_k2, subln, proj_a,
              s5_lam_re, s5_lam_im, s5_log_dt, s5_b_re, s5_b_im, s5_c_re, s5_c_im,
              s5_d, s5_w_glu, s5_b_glu, proj_b,
              rw_mu, rw_w0, rw_w2, rw_a0, rw_a2, rw_g2, rw_k_k, rw_k_a, rw_r_k,
              rw_ln_w, rw_ln_b, proj_c, w_out,
              norm_x, norm_mem, w_xq, w_mk, w_mv, w_xo,
              norm_ffn, w_gate, w_up, w_down, norm_final):
    bp = x_prompt.shape[0]
    zeros_ssm = jnp.zeros((bp, S5_GROUPS, S5_STATE), x_prompt.dtype)
    zeros_rwkv = jnp.zeros((bp, HC, NC, NC), x_prompt.dtype)
    zeros_shift = jnp.zeros((bp, C_PROJ), x_prompt.dtype)
    xp, xs = x_prompt, x_sample
    kp_l, vp_l, mkp_l, mvp_l, srp_l, sip_l, rwp_l, shp_l = [], [], [], [], [], [], [], []
    ks_l, vs_l, srs_l, sis_l, rws_l, shs_l = [], [], [], [], [], []
    for l in range(DEPTH):
        lp = dict(norm_mix=norm_mix[l], w_in=w_in[l], lam_q1=lam_q1[l], lam_k1=lam_k1[l],
                  lam_q2=lam_q2[l], lam_k2=lam_k2[l], subln=subln[l], proj_a=proj_a[l],
                  s5_lam_re=s5_lam_re[l], s5_lam_im=s5_lam_im[l], s5_log_dt=s5_log_dt[l],
                  s5_b_re=s5_b_re[l], s5_b_im=s5_b_im[l], s5_c_re=s5_c_re[l], s5_c_im=s5_c_im[l],
                  s5_d=s5_d[l], s5_w_glu=s5_w_glu[l], s5_b_glu=s5_b_glu[l], proj_b=proj_b[l],
                  rw_mu=rw_mu[l], rw_w0=rw_w0[l], rw_w2=rw_w2[l], rw_a0=rw_a0[l], rw_a2=rw_a2[l],
                  rw_g2=rw_g2[l], rw_k_k=rw_k_k[l], rw_k_a=rw_k_a[l], rw_r_k=rw_r_k[l],
                  rw_ln_w=rw_ln_w[l], rw_ln_b=rw_ln_b[l], proj_c=proj_c[l], w_out=w_out[l],
                  norm_x=norm_x[l], norm_mem=norm_mem[l], w_xq=w_xq[l], w_mk=w_mk[l],
                  w_mv=w_mv[l], w_xo=w_xo[l], norm_ffn=norm_ffn[l], w_gate=w_gate[l],
                  w_up=w_up[l], w_down=w_down[l])
        lam_init = 0.8 - 0.6 * math.exp(-0.3 * l)
        mk_p, mv_p = memory_kv(mem_prompt, lp)
        xp, kp, vp, srp, sip, rwp, shp = trunk_layer(
            xp, diff_attn_prompt, mk_p, mv_p, zeros_ssm, zeros_ssm, zeros_rwkv, zeros_shift, lp, lam_init)
        attn_s = functools.partial(diff_attn_sample, cache_k[l], cache_v[l], page_table)
        xs, kk_s, vv_s, srs, sis, rws, shs = trunk_layer(
            xs, attn_s, cache_mem_k[l], cache_mem_v[l], state_ssm_re[l], state_ssm_im[l],
            state_rwkv[l], state_shift[l], lp, lam_init)
        kp_l.append(kp); vp_l.append(vp); mkp_l.append(mk_p); mvp_l.append(mv_p)
        srp_l.append(srp); sip_l.append(sip); rwp_l.append(rwp); shp_l.append(shp)
        ks_l.append(kk_s); vs_l.append(vv_s); srs_l.append(srs); sis_l.append(sis)
        rws_l.append(rws); shs_l.append(shs)
    y_prompt = rmsnorm(xp, norm_final)
    y_sample = rmsnorm(xs, norm_final)
    return (y_prompt, y_sample,
            jnp.stack(kp_l), jnp.stack(vp_l), jnp.stack(mkp_l), jnp.stack(mvp_l),
            jnp.stack(srp_l), jnp.stack(sip_l), jnp.stack(rwp_l), jnp.stack(shp_l),
            jnp.stack(ks_l), jnp.stack(vs_l), jnp.stack(srs_l), jnp.stack(sis_l),
            jnp.stack(rws_l), jnp.stack(shs_l))
```

```python
import functools
import math

import jax
import jax.numpy as jnp
from jax import lax
from jax.experimental import pallas as pl
from jax.experimental.pallas import tpu as pltpu

F32 = jnp.float32
BF16 = jnp.bfloat16

D_MODEL = 1024
PAGE_SIZE = 128
HA, DK = 4, 64
DV = 2 * DK
A_WIDTH = HA * DV
S5_GROUP, S5_STATE, S5_WIDTH = 16, 64, 512
S5_GROUPS = S5_WIDTH // S5_GROUP
S5_LANES = S5_GROUPS * S5_STATE
HC, NC = 8, 64
C_WIDTH = HC * NC
DECAY_LORA, AAA_LORA, GATE_LORA = 64, 64, 128
C_PROJ = 3 * C_WIDTH + DECAY_LORA + AAA_LORA + GATE_LORA
N_MEM, HX, DX = 256, 4, 64
X_WIDTH = HX * DX
RMS_EPS = 1e-6
GN_EPS = 64e-5
NEG_INF = -1e30

V7X_SUBLANES = 8
V7X_VMEM_BYTES = 64 * 1024 * 1024
VMEM_LIMIT = V7X_VMEM_BYTES - 8 * 1024 * 1024

ROW_TILE = 256
ATT_TQ = 512
ATT_TK = 512
PAGES_PER_STEP = 8
S5_TILE = 256
S5_LANE_CHUNK = 512
RW_CHUNK = 64


def _params(*sem):
    return pltpu.CompilerParams(dimension_semantics=sem, vmem_limit_bytes=VMEM_LIMIT)


def _resident(shape):
    nd = len(shape)
    return pl.BlockSpec(shape, lambda *_: (0,) * nd, pipeline_mode=pl.Buffered(1))


def _rms(x, g):
    return x * lax.rsqrt(jnp.mean(x * x, axis=-1, keepdims=True) + RMS_EPS) * g


def _dot(a, b):
    return jnp.dot(a, b, preferred_element_type=F32)


def _dot_nt(a, b):
    return lax.dot_general(a, b, (((1,), (1,)), ((), ())), preferred_element_type=F32)


def _dot_tn(a, b):
    return lax.dot_general(a, b, (((0,), (0,)), ((), ())), preferred_element_type=F32)


def _split3(x):
    hi = x.astype(BF16)
    r1 = x - hi.astype(F32)
    mid = r1.astype(BF16)
    lo = (r1 - mid.astype(F32)).astype(BF16)
    return hi, mid, lo


def _dot_exact_rhs(sel, x):
    hi, mid, lo = _split3(x)
    return _dot(sel, hi) + _dot(sel, mid) + _dot(sel, lo)


def _dot_exact_lhs(x, sel):
    hi, mid, lo = _split3(x)
    return _dot(hi, sel) + _dot(mid, sel) + _dot(lo, sel)


def _norm_proj_body(splits, x_ref, g_ref, w_ref, *out_refs):
    hb = _rms(x_ref[...], g_ref[...]).astype(BF16)
    k = 0
    for off, width, outs in splits:
        y = _dot(hb, w_ref[:, off:off + width])
        for fn, _ in outs:
            out_refs[k][...] = fn(y).astype(out_refs[k].dtype)
            k += 1


def _norm_proj(x, g, w, splits, tm):
    n, d = x.shape
    out_shape, out_specs = [], []
    for _, width, outs in splits:
        for _, dt in outs:
            out_shape.append(jax.ShapeDtypeStruct((n, width), dt))
            out_specs.append(pl.BlockSpec((tm, width), lambda i: (i, 0)))
    return pl.pallas_call(
        functools.partial(_norm_proj_body, splits),
        grid=(n // tm,),
        in_specs=[pl.BlockSpec((tm, d), lambda i: (i, 0)), _resident(g.shape), _resident(w.shape)],
        out_specs=out_specs,
        out_shape=out_shape,
        compiler_params=_params("parallel"),
        name="norm_proj",
    )(x, g, w)


def _mem_kv_body(x_ref, g_ref, wt_ref, o_ref):
    hb = _rms(x_ref[...], g_ref[...]).astype(BF16)
    o_ref[...] = _dot_nt(wt_ref[...], hb)


def _mem_kv(mem, g, wt):
    return pl.pallas_call(
        _mem_kv_body,
        out_shape=jax.ShapeDtypeStruct((wt.shape[0], mem.shape[0]), F32),
        compiler_params=pltpu.CompilerParams(vmem_limit_bytes=VMEM_LIMIT),
        name="mem_kv",
    )(mem, g, wt)


def _ident(y):
    return y


def _scale_q(y):
    return y * (DK ** -0.5)


def _attn_prompt_body(post_scale, lam_ref, q_ref, k_ref, v_ref, g_ref, o_ref, m_sc, l_sc, acc_sc):
    i = pl.program_id(1)
    j = pl.program_id(2)
    tq, tk = q_ref.shape[0], k_ref.shape[0]

    @pl.when(j == 0)
    def _():
        m_sc[...] = jnp.full(m_sc.shape, NEG_INF, F32)
        l_sc[...] = jnp.zeros(l_sc.shape, F32)
        acc_sc[...] = jnp.zeros(acc_sc.shape, F32)

    @pl.when(j <= i)
    def _():
        q = q_ref[...]
        k = k_ref[...]
        v = v_ref[...]
        row = i * tq + lax.broadcasted_iota(jnp.int32, (tq, tk), 0)
        col = j * tk + lax.broadcasted_iota(jnp.int32, (tq, tk), 1)
        keep = col <= row
        for c in range(2):
            s = _dot_nt(q[:, c * DK:(c + 1) * DK], k[:, c * DK:(c + 1) * DK])
            s = jnp.where(keep, s, NEG_INF)
            m_prev = m_sc[c]
            m_new = jnp.maximum(m_prev, jnp.max(s, axis=-1, keepdims=True))
            alpha = jnp.exp(m_prev - m_new)
            p = jnp.exp(s - m_new)
            l_sc[c] = alpha * l_sc[c] + jnp.sum(p, axis=-1, keepdims=True)
            acc_sc[c] = alpha * acc_sc[c] + _dot(p.astype(BF16), v)
            m_sc[c] = m_new

    @pl.when(j == i)
    def _():
        lam = lam_ref[0]
        o = acc_sc[0] / l_sc[0] - lam * (acc_sc[1] / l_sc[1])
        o_ref[...] = (_rms(o, g_ref[...]) * post_scale).astype(o_ref.dtype)


def _attn_prompt(lam, q, k, v, subln, post_scale):
    n = q.shape[0]
    tq, tk = min(ATT_TQ, n), min(ATT_TK, n)
    assert tq == tk
    kv_spec = pl.BlockSpec((tk, DV), lambda h, i, j: (jnp.minimum(j, i), h))
    return pl.pallas_call(
        functools.partial(_attn_prompt_body, post_scale),
        grid=(HA, n // tq, n // tk),
        in_specs=[
            pl.BlockSpec(memory_space=pltpu.SMEM),
            pl.BlockSpec((tq, DV), lambda h, i, j: (i, h)),
            kv_spec, kv_spec,
            pl.BlockSpec((1, DV), lambda h, i, j: (0, 0)),
        ],
        out_specs=pl.BlockSpec((tq, DV), lambda h, i, j: (i, h)),
        out_shape=jax.ShapeDtypeStruct((n, A_WIDTH), BF16),
        scratch_shapes=[pltpu.VMEM((2, tq, 1), F32), pltpu.VMEM((2, tq, 1), F32),
                        pltpu.VMEM((2, tq, DV), F32)],
        compiler_params=_params("parallel", "parallel", "arbitrary"),
        name="attn_prompt",
    )(lam, q, k, v, subln)


def _attn_sample_body(post_scale, n_tok, pt_ref, lam_ref, q_ref, kn_ref, vn_ref, g_ref, *rest):
    del pt_ref
    npg = PAGES_PER_STEP
    k_refs, v_refs = rest[:npg], rest[npg:2 * npg]
    o_ref, m_sc, l_sc, acc_sc = rest[2 * npg:]
    j = pl.program_id(1)
    hrows = 2 * n_tok
    nrow = HA * hrows
    width = HA * 2 * DK

    rid = lax.broadcasted_iota(jnp.int32, (nrow, width), 0)
    cid = lax.broadcasted_iota(jnp.int32, (nrow, width), 1)
    qrep = jnp.concatenate([q_ref[...]] * (2 * HA), axis=0)
    qblk = jnp.where(cid // DK == rid // n_tok, qrep, jnp.zeros_like(qrep)).astype(BF16)

    @pl.when(j == 0)
    def _():
        m_sc[...] = jnp.full(m_sc.shape, NEG_INF, F32)
        l_sc[...] = jnp.zeros(l_sc.shape, F32)
        acc_sc[...] = jnp.zeros(acc_sc.shape, F32)

    def update(s, v_heads):
        m_prev = m_sc[...]
        m_new = jnp.maximum(m_prev, jnp.max(s, axis=-1, keepdims=True))
        alpha = jnp.exp(m_prev - m_new)
        p = jnp.exp(s - m_new)
        pb = p.astype(BF16)
        pv = jnp.concatenate([_dot(pb[h * hrows:(h + 1) * hrows], v_heads[h]) for h in range(HA)], axis=0)
        l_sc[...] = alpha * l_sc[...] + jnp.sum(p, axis=-1, keepdims=True)
        acc_sc[...] = alpha * acc_sc[...] + pv
        m_sc[...] = m_new

    for i in range(npg):
        s = _dot(qblk, k_refs[i][...].astype(BF16))
        update(s, [v_refs[i][pl.ds(h, PAGE_SIZE, stride=HA), :].astype(BF16) for h in range(HA)])

    @pl.when(j == pl.num_programs(1) - 1)
    def _():
        pad = jnp.zeros((PAGE_SIZE - n_tok, width), F32)
        kn = jnp.concatenate([kn_ref[...], pad], axis=0).astype(BF16)
        vn = jnp.concatenate([vn_ref[...], pad], axis=0).astype(BF16)
        s = _dot_nt(qblk, kn)
        r2 = lax.broadcasted_iota(jnp.int32, (nrow, PAGE_SIZE), 0) % n_tok
        c2 = lax.broadcasted_iota(jnp.int32, (nrow, PAGE_SIZE), 1)
        s = jnp.where(c2 <= r2, s, NEG_INF)
        update(s, [vn[:, h * DV:(h + 1) * DV] for h in range(HA)])
        sel = acc_sc[...] / l_sc[...]
        lam = lam_ref[0]
        heads = []
        for h in range(HA):
            o = sel[h * hrows:h * hrows + n_tok] - lam * sel[h * hrows + n_tok:(h + 1) * hrows]
            heads.append(_rms(o, g_ref[...]) * post_scale)
        o_ref[...] = jnp.concatenate(heads, axis=1).astype(o_ref.dtype)


def _attn_sample(layer, page_table, lam, q, k_new, v_new, subln, cache_kt, cache_vr, post_scale):
    b, n_tok, width = q.shape
    n_pages = page_table.shape[1]
    npg = PAGES_PER_STEP
    nj = n_pages // npg
    pt = page_table.reshape(-1)

    def page_spec(i):
        return pl.BlockSpec((None, None, width, PAGE_SIZE),
                            lambda bb, j, p: (layer, p[bb * n_pages + j * npg + i], 0, 0))

    tok_spec = pl.BlockSpec((None, n_tok, width), lambda bb, j, p: (bb, 0, 0))
    nrow = 2 * HA * n_tok
    grid_spec = pltpu.PrefetchScalarGridSpec(
        num_scalar_prefetch=1,
        grid=(b, nj),
        in_specs=[pl.BlockSpec(memory_space=pltpu.SMEM), tok_spec, tok_spec, tok_spec,
                  pl.BlockSpec((1, DV), lambda bb, j, p: (0, 0))]
                 + [page_spec(i) for i in range(npg)] + [page_spec(i) for i in range(npg)],
        out_specs=tok_spec,
        scratch_shapes=[pltpu.VMEM((nrow, 1), F32), pltpu.VMEM((nrow, 1), F32),
                        pltpu.VMEM((nrow, DV), F32)],
    )
    return pl.pallas_call(
        functools.partial(_attn_sample_body, post_scale, n_tok),
        grid_spec=grid_spec,
        out_shape=jax.ShapeDtypeStruct((b, n_tok, width), F32),
        compiler_params=_params("parallel", "arbitrary"),
        name="attn_sample",
    )(pt, lam, q, k_new, v_new, subln, *([cache_kt] * npg), *([cache_vr] * npg))


def _s5_body(chained, u_ref, x0r_ref, x0i_ref, tab_ref, bre_ref, bim_ref, cre_ref, cim_ref,
             d_ref, wg_ref, bg_ref, y_ref, xfr_ref, xfi_ref, sr_sc, si_sc, st_sc):
    i = pl.program_id(0)
    tt = u_ref.shape[0]
    sub = V7X_SUBLANES
    u = u_ref[...]
    ub = u.astype(BF16)
    sr_sc[sub:, :] = _dot(ub, bre_ref[...])
    si_sc[sub:, :] = _dot(ub, bim_ref[...])

    if chained:
        @pl.when(i == 0)
        def _():
            st_sc[0:1, :] = x0r_ref[...]
            st_sc[1:2, :] = x0i_ref[...]

        sr_sc[sub - 1:sub, :] = st_sc[0:1, :]
        si_sc[sub - 1:sub, :] = st_sc[1:2, :]

    def group(gi, carry):
        r0 = pl.multiple_of(gi * sub, sub)
        for lc in range(S5_LANES // S5_LANE_CHUNK):
            ls = slice(lc * S5_LANE_CHUNK, (lc + 1) * S5_LANE_CHUNK)
            vr = sr_sc[pl.ds(r0 + sub, sub), ls]
            vi = si_sc[pl.ds(r0 + sub, sub), ls]
            for t, d in enumerate((1, 2, 4)):
                ar, ai = tab_ref[2 * t, :, ls], tab_ref[2 * t + 1, :, ls]
                pr, pi = pltpu.roll(vr, d, axis=0), pltpu.roll(vi, d, axis=0)
                vr, vi = vr + ar * pr - ai * pi, vi + ar * pi + ai * pr
            if chained:
                cr = sr_sc[pl.ds(r0 + sub - 1, 1), ls]
                ci = si_sc[pl.ds(r0 + sub - 1, 1), ls]
            else:
                cr = x0r_ref[pl.ds(gi, 1), ls]
                ci = x0i_ref[pl.ds(gi, 1), ls]
            cr = jnp.broadcast_to(cr, vr.shape)
            ci = jnp.broadcast_to(ci, vi.shape)
            pr, pi = tab_ref[6, :, ls], tab_ref[7, :, ls]
            vr, vi = vr + pr * cr - pi * ci, vi + pr * ci + pi * cr
            sr_sc[pl.ds(r0 + sub, sub), ls] = vr
            si_sc[pl.ds(r0 + sub, sub), ls] = vi
            if not chained:
                xfr_ref[pl.ds(gi, 1), ls] = vr[sub - 1:sub]
                xfi_ref[pl.ds(gi, 1), ls] = vi[sub - 1:sub]
        return carry

    lax.fori_loop(0, tt // sub, group, 0)

    if chained:
        st_sc[0:1, :] = sr_sc[tt + sub - 1:tt + sub, :]
        st_sc[1:2, :] = si_sc[tt + sub - 1:tt + sub, :]
        xfr_ref[...] = st_sc[0:1, :]
        xfi_ref[...] = st_sc[1:2, :]

    xr = sr_sc[sub:, :].astype(BF16)
    xi = si_sc[sub:, :].astype(BF16)
    y = _dot(xr, cre_ref[...]) - _dot(xi, cim_ref[...]) + d_ref[...] * u
    z = y * (0.5 * (1.0 + jnp.tanh(math.sqrt(2.0 / math.pi) * (y + 0.044715 * (y * y * y)))))
    gate = jax.nn.sigmoid(_dot(z.astype(BF16), wg_ref[...]) + bg_ref[...])
    y_ref[...] = (z * gate).astype(y_ref.dtype)


def _s5(u, x0r, x0i, sp, chained):
    n = u.shape[0]
    tt = min(S5_TILE, n)
    ng = x0r.shape[0]
    if chained:
        st_spec = pl.BlockSpec((1, S5_LANES), lambda i: (0, 0))
    else:
        assert tt == n and ng * V7X_SUBLANES == n
        st_spec = pl.BlockSpec((ng, S5_LANES), lambda i: (0, 0))
    consts = [sp["tab"], sp["b_re"], sp["b_im"], sp["c_re"], sp["c_im"], sp["d"], sp["w_glu"], sp["b_glu"]]
    return pl.pallas_call(
        functools.partial(_s5_body, chained),
        grid=(n // tt,),
        in_specs=[pl.BlockSpec((tt, S5_WIDTH), lambda i: (i, 0)), st_spec, st_spec]
                 + [_resident(c.shape) for c in consts],
        out_specs=[pl.BlockSpec((tt, S5_WIDTH), lambda i: (i, 0)), st_spec, st_spec],
        out_shape=[jax.ShapeDtypeStruct((n, S5_WIDTH), BF16),
                   jax.ShapeDtypeStruct((ng, S5_LANES), F32),
                   jax.ShapeDtypeStruct((ng, S5_LANES), F32)],
        scratch_shapes=[pltpu.VMEM((tt + V7X_SUBLANES, S5_LANES), F32),
                        pltpu.VMEM((tt + V7X_SUBLANES, S5_LANES), F32),
                        pltpu.VMEM((2, S5_LANES), F32)],
        compiler_params=_params("arbitrary"),
        name="s5_branch",
    )(u, x0r, x0i, *consts)


def _rwkv_body(t_valid, pc_ref, prev_ref, sh_ref, s0_ref, mu_ref, w0_ref, w2_ref, a0_ref, a2_ref,
               g2_ref, kk_ref, ka_ref, rk_ref, lnw_ref, lnb_ref, ones_ref,
               y_ref, sf_ref, s_sc):
    j = pl.program_id(1)
    t = RW_CHUNK
    rows_in = pc_ref.shape[0]

    @pl.when(j == 0)
    def _():
        s_sc[...] = s0_ref[...]

    pc = pc_ref[...]
    if rows_in < t:
        pc = jnp.concatenate([pc, jnp.zeros((t - rows_in, C_PROJ), F32)], axis=0)
    prev_row = jnp.where(j == 0, sh_ref[...], prev_ref[V7X_SUBLANES - 1:V7X_SUBLANES, :])
    row_id = lax.broadcasted_iota(jnp.int32, (t, C_PROJ), 0)
    shifted = jnp.where(row_id == 0, jnp.broadcast_to(prev_row, pc.shape), pltpu.roll(pc, 1, axis=0))
    xm = pc + (shifted - pc) * mu_ref[...]

    c = C_WIDTH
    r = xm[:, 0:c]
    k = xm[:, c:2 * c]
    v = xm[:, 2 * c:3 * c]
    pw = xm[:, 3 * c:3 * c + DECAY_LORA]
    pa = xm[:, 3 * c + DECAY_LORA:3 * c + DECAY_LORA + AAA_LORA]
    pg = xm[:, 3 * c + DECAY_LORA + AAA_LORA:]

    wl = w0_ref[...] + _dot(jnp.tanh(pw).astype(BF16), w2_ref[...])
    neg = -wl
    softplus = jnp.maximum(neg, 0.0) + jnp.log1p(jnp.exp(-jnp.abs(neg)))
    logw = -jnp.exp(-softplus - 0.5)
    a = jax.nn.sigmoid(a0_ref[...] + _dot(pa.astype(BF16), a2_ref[...]))
    g = _dot(jax.nn.sigmoid(pg).astype(BF16), g2_ref[...])

    ones = ones_ref[...]
    kk = k * kk_ref[...]
    kk = kk / jnp.maximum(jnp.sqrt(_dot_exact_lhs(kk * kk, ones)), 1e-12)
    kh = k * (1.0 + (a - 1.0) * ka_ref[...])
    alpha = -kk
    beta = kk * a

    if t_valid < t:
        valid = lax.broadcasted_iota(jnp.int32, (t, c), 0) < t_valid
        zero = jnp.zeros((t, c), F32)
        logw = jnp.where(valid, logw, zero)
        alpha = jnp.where(valid, alpha, zero)
        beta = jnp.where(valid, beta, zero)
        kh_m = jnp.where(valid, kh, zero)
        v_m = jnp.where(valid, v, zero)
    else:
        kh_m, v_m = kh, v

    ti = lax.broadcasted_iota(jnp.int32, (t, t), 0)
    si = lax.broadcasted_iota(jnp.int32, (t, t), 1)
    incl = si <= ti
    strict = si < ti
    cum = _dot_exact_rhs(incl.astype(BF16), logw)
    c_end = cum[t - 1:t, :]
    e_cum = jnp.exp(cum)
    e_neg = jnp.exp(-cum)
    e_tail = jnp.exp(c_end - cum)
    at = alpha * jnp.exp(cum - logw)
    rt = r * e_cum
    bt = beta * e_neg
    kt = kh_m * e_neg
    bh = beta * e_tail
    kx = kh_m * e_tail
    w_end = jnp.exp(c_end)
    eye = (si == ti).astype(F32)

    ys = []
    for h in range(HC):
        hs = slice(h * NC, (h + 1) * NC)
        s0 = s_sc[h]
        s0b = s0.astype(BF16)
        lhs = jnp.concatenate([at[:, hs], rt[:, hs]], axis=0).astype(BF16)
        rhs = jnp.concatenate([bt[:, hs], kt[:, hs]], axis=0).astype(BF16)
        blk = _dot_nt(lhs, rhs)
        n_ab = jnp.where(strict, blk[:t, :t], 0.0)
        a_ak = jnp.where(strict, blk[:t, t:], 0.0)
        a_br = jnp.where(incl, blk[t:, :t], 0.0)
        a_kr = jnp.where(incl, blk[t:, t:], 0.0)
        minv = eye + n_ab
        pw2 = n_ab
        span = 1
        while span * 2 < t:
            pb = pw2.astype(BF16)
            pw2 = _dot(pb, pb)
            minv = minv + _dot(pw2.astype(BF16), minv.astype(BF16))
            span *= 2
        vb = v_m[:, hs].astype(BF16)
        x = _dot_nt(at[:, hs].astype(BF16), s0b) + _dot(a_ak.astype(BF16), vb)
        ut = _dot(minv.astype(BF16), x.astype(BF16))
        utb = ut.astype(BF16)
        ys.append(_dot_nt(rt[:, hs].astype(BF16), s0b) + _dot(a_br.astype(BF16), utb)
                  + _dot(a_kr.astype(BF16), vb))
        s_new = (s0 * w_end[:, hs] + _dot_tn(utb, bh[:, hs].astype(BF16))
                 + _dot_tn(vb, kx[:, hs].astype(BF16)))
        s_sc[h] = s_new
        sf_ref[h] = s_new

    y = jnp.concatenate(ys, axis=1)
    inv_n = 1.0 / NC
    mean = _dot_exact_lhs(y, ones) * inv_n
    dlt = y - mean
    var = _dot_exact_lhs(dlt * dlt, ones) * inv_n
    yn = dlt * lax.rsqrt(var + GN_EPS) * lnw_ref[...] + lnb_ref[...]
    bonus = _dot_exact_lhs(r * kh * rk_ref[...], ones) * v
    out = (yn + bonus) * g
    y_ref[...] = out[:rows_in].astype(y_ref.dtype)


def _rwkv(pc, shift_prev, s0, rp, batch, seq):
    t = RW_CHUNK
    rows = min(t, seq)
    nt = seq // rows
    sub = V7X_SUBLANES
    per8 = seq // sub
    consts = [rp["mu"], rp["w0"], rp["w2"], rp["a0"], rp["a2"], rp["g2"], rp["k_k"], rp["k_a"],
              rp["r_k"], rp["ln_w"], rp["ln_b"], rp["ones"]]
    return pl.pallas_call(
        functools.partial(_rwkv_body, rows),
        grid=(batch, nt),
        in_specs=[
            pl.BlockSpec((rows, C_PROJ), lambda b, j: (b * nt + j, 0)),
            pl.BlockSpec((sub, C_PROJ), lambda b, j: (b * per8 + jnp.maximum(j * (rows // sub) - 1, 0), 0)),
            pl.BlockSpec((None, 1, C_PROJ), lambda b, j: (b, 0, 0)),
            pl.BlockSpec((None, HC, NC, NC), lambda b, j: (b, 0, 0, 0)),
        ] + [_resident(c.shape) for c in consts],
        out_specs=[pl.BlockSpec((rows, C_WIDTH), lambda b, j: (b * nt + j, 0)),
                   pl.BlockSpec((None, HC, NC, NC), lambda b, j: (b, 0, 0, 0))],
        out_shape=[jax.ShapeDtypeStruct((batch * seq, C_WIDTH), F32),
                   jax.ShapeDtypeStruct((batch, HC, NC, NC), F32)],
        scratch_shapes=[pltpu.VMEM((HC, NC, NC), F32)],
        compiler_params=_params("parallel", "arbitrary"),
        name="rwkv_branch",
    )(pc, pc, shift_prev, s0, *consts)


def _merge_body(x_ref, sg_ref, ya_ref, yb_ref, yc_ref, pa_ref, pb_ref, pc_ref, wo_ref, o_ref):
    d = D_MODEL
    sg = sg_ref[...]
    merged = (sg[:, 0:d] * _dot(ya_ref[...].astype(BF16), pa_ref[...])
              + sg[:, d:2 * d] * _dot(yb_ref[...].astype(BF16), pb_ref[...])
              + sg[:, 2 * d:3 * d] * _dot(yc_ref[...].astype(BF16), pc_ref[...]))
    o_ref[...] = x_ref[...] + _dot(merged.astype(BF16), wo_ref[...])


def _merge(x, sg, ya, yb, yc, wp, tm):
    n = x.shape[0]
    rows = lambda w: pl.BlockSpec((tm, w), lambda i: (i, 0))
    consts = [wp["proj_a"], wp["proj_b"], wp["proj_c"], wp["w_out"]]
    return pl.pallas_call(
        _merge_body,
        grid=(n // tm,),
        in_specs=[rows(D_MODEL), rows(3 * D_MODEL), rows(A_WIDTH), rows(S5_WIDTH), rows(C_WIDTH)]
                 + [_resident(c.shape) for c in consts],
        out_specs=rows(D_MODEL),
        out_shape=jax.ShapeDtypeStruct((n, D_MODEL), F32),
        compiler_params=_params("parallel"),
        name="merge_out",
    )(x, sg, ya, yb, yc, *consts)


def _cross_body(x_ref, g_ref, wq_ref, mk_ref, mv_ref, wo_ref, o_ref):
    x = x_ref[...]
    q = _dot(_rms(x, g_ref[...]).astype(BF16), wq_ref[...])
    mk = mk_ref[...].astype(BF16)
    mv = mv_ref[...].astype(BF16)
    heads = []
    for h in range(HX):
        hs = slice(h * DX, (h + 1) * DX)
        s = _dot(q[:, hs].astype(BF16), mk[hs, :]) * (DX ** -0.5)
        e = jnp.exp(s - jnp.max(s, axis=-1, keepdims=True))
        p = e / jnp.sum(e, axis=-1, keepdims=True)
        heads.append(_dot_nt(p.astype(BF16), mv[hs, :]))
    o = jnp.concatenate(heads, axis=1)
    o_ref[...] = x + _dot(o.astype(BF16), wo_ref[...])


def _cross(x, mem_k, mem_v, wp, batch, seq, tm):
    nt = seq // tm
    xs = pl.BlockSpec((tm, D_MODEL), lambda b, j: (b * nt + j, 0))
    ms = pl.BlockSpec((None, X_WIDTH, N_MEM), lambda b, j: (b, 0, 0))
    consts_a = [wp["norm_x"], wp["w_xq"]]
    return pl.pallas_call(
        _cross_body,
        grid=(batch, nt),
        in_specs=[xs] + [_resident(c.shape) for c in consts_a] + [ms, ms, _resident(wp["w_xo"].shape)],
        out_specs=xs,
        out_shape=jax.ShapeDtypeStruct((batch * seq, D_MODEL), F32),
        compiler_params=_params("parallel", "parallel"),
        name="cross_attn",
    )(x, *consts_a, mem_k, mem_v, wp["w_xo"])


def _ffn_body(x_ref, g_ref, wg_ref, wu_ref, wd_ref, o_ref):
    x = x_ref[...]
    hb = _rms(x, g_ref[...]).astype(BF16)
    gate = _dot(hb, wg_ref[...])
    up = _dot(hb, wu_ref[...])
    act = (gate * jax.nn.sigmoid(gate) * up).astype(BF16)
    o_ref[...] = x + _dot(act, wd_ref[...])


def _ffn(x, wp, tm):
    n = x.shape[0]
    rows = pl.BlockSpec((tm, D_MODEL), lambda i: (i, 0))
    consts = [wp["norm_ffn"], wp["w_gate"], wp["w_up"], wp["w_down"]]
    return pl.pallas_call(
        _ffn_body,
        grid=(n // tm,),
        in_specs=[rows] + [_resident(c.shape) for c in consts],
        out_specs=rows,
        out_shape=jax.ShapeDtypeStruct((n, D_MODEL), F32),
        compiler_params=_params("parallel"),
        name="swiglu",
    )(x, *consts)


def _final_norm_body(x_ref, g_ref, o_ref):
    o_ref[...] = _rms(x_ref[...], g_ref[...])


def _final_norm(x, g, tm):
    n = x.shape[0]
    rows = pl.BlockSpec((tm, D_MODEL), lambda i: (i, 0))
    return pl.pallas_call(
        _final_norm_body,
        grid=(n // tm,),
        in_specs=[rows, _resident(g.shape)],
        out_specs=rows,
        out_shape=jax.ShapeDtypeStruct((n, D_MODEL), F32),
        compiler_params=_params("parallel"),
        name="final_norm",
    )(x, g)


def _s5_params(lp):
    lam_re, lam_im = lp["s5_lam_re"], lp["s5_lam_im"]
    dt = jnp.exp(lp["s5_log_dt"])[:, None]
    mag = jnp.exp(lam_re * dt)
    ab_re = mag * jnp.cos(lam_im * dt)
    ab_im = mag * jnp.sin(lam_im * dt)
    den = lam_re * lam_re + lam_im * lam_im
    nr = ab_re - 1.0
    cr = (nr * lam_re + ab_im * lam_im) / den
    ci = (ab_im * lam_re - nr * lam_im) / den
    bb_re = cr[..., None] * lp["s5_b_re"] - ci[..., None] * lp["s5_b_im"]
    bb_im = cr[..., None] * lp["s5_b_im"] + ci[..., None] * lp["s5_b_re"]
    eye = jnp.eye(S5_GROUPS, dtype=F32)

    def blockdiag_in(b):
        return jnp.einsum("gph,gk->ghkp", b, eye).reshape(S5_WIDTH, S5_LANES)

    def blockdiag_out(cm):
        return jnp.einsum("ghp,gk->gpkh", cm, eye).reshape(S5_LANES, S5_WIDTH)

    ar, ai = ab_re.reshape(-1), ab_im.reshape(-1)
    pows_r, pows_i = [ar], [ai]
    for _ in range(V7X_SUBLANES - 1):
        pr, pi = pows_r[-1], pows_i[-1]
        pows_r.append(pr * ar - pi * ai)
        pows_i.append(pr * ai + pi * ar)
    rows = jnp.arange(V7X_SUBLANES)[:, None]
    tabs = []
    for d in (1, 2, 4):
        tabs.append(jnp.where(rows >= d, pows_r[d - 1][None, :], 0.0))
        tabs.append(jnp.where(rows >= d, pows_i[d - 1][None, :], 0.0))
    tabs.append(jnp.stack(pows_r))
    tabs.append(jnp.stack(pows_i))
    return {
        "tab": jnp.stack(tabs).astype(F32),
        "b_re": blockdiag_in(bb_re).astype(BF16), "b_im": blockdiag_in(bb_im).astype(BF16),
        "c_re": blockdiag_out(lp["s5_c_re"]).astype(BF16), "c_im": blockdiag_out(lp["s5_c_im"]).astype(BF16),
        "d": lp["s5_d"][None, :], "w_glu": lp["s5_w_glu"].astype(BF16), "b_glu": lp["s5_b_glu"][None, :],
    }


def _rwkv_params(lp):
    seg = jnp.arange(C_WIDTH) // NC
    row = lambda a: a.reshape(1, -1)
    return {
        "mu": row(lp["rw_mu"]), "w0": row(lp["rw_w0"]), "w2": lp["rw_w2"].astype(BF16),
        "a0": row(lp["rw_a0"]), "a2": lp["rw_a2"].astype(BF16), "g2": lp["rw_g2"].astype(BF16),
        "k_k": row(lp["rw_k_k"]), "k_a": row(lp["rw_k_a"]), "r_k": row(lp["rw_r_k"]),
        "ln_w": row(lp["rw_ln_w"]), "ln_b": row(lp["rw_ln_b"]),
        "ones": (seg[:, None] == seg[None, :]).astype(BF16),
    }


def _in_proj(x, lp, tm):
    qa = HA * 2 * DK
    o_u = 3 * qa
    o_pc = o_u + S5_WIDTH
    o_g = o_pc + C_PROJ
    splits = [
        (0, qa, [(_scale_q, BF16)]),
        (qa, qa, [(_ident, F32), (_ident, BF16)]),
        (2 * qa, qa, [(_ident, F32), (_ident, BF16)]),
        (o_u, S5_WIDTH, [(_ident, F32)]),
        (o_pc, C_PROJ, [(_ident, F32)]),
        (o_g, 3 * D_MODEL, [(jax.nn.sigmoid, F32)]),
    ]
    return _norm_proj(x, lp["norm_mix"], lp["w_in_b"], splits, tm)


def _layer(l, x, batch, seq, attn_fn, mem_k, mem_v, ssm_re0, ssm_im0, rwkv0, shift0, lp):
    n = batch * seq
    tm = min(ROW_TILE, n)
    lam_init = 0.8 - 0.6 * math.exp(-0.3 * l)
    lam = (jnp.exp(jnp.sum(lp["lam_q1"] * lp["lam_k1"])) - jnp.exp(jnp.sum(lp["lam_q2"] * lp["lam_k2"]))
           + lam_init).reshape(1).astype(F32)
    q_b, k_f, k_b, v_f, v_b, u, pc, sg = _in_proj(x, lp, tm)
    ya = attn_fn(lam, q_b, k_f, k_b, v_f, v_b, lp["subln"], 1.0 - lam_init)
    yb, ssm_re, ssm_im = _s5(u, ssm_re0, ssm_im0, lp["s5"], chained=(batch == 1))
    yc, rwkv_s = _rwkv(pc, shift0, rwkv0, lp["rw"], batch, seq)
    shift = pc.reshape(batch, seq, C_PROJ)[:, -1]
    x = _merge(x, sg, ya, yb, yc, lp, tm)
    x = _cross(x, mem_k, mem_v, lp, batch, seq, min(tm, seq))
    x = _ffn(x, lp, tm)
    return x, k_f, v_f, ssm_re, ssm_im, rwkv_s, shift


def kernel(x_prompt, x_sample, cache_k, cache_v, cache_mem_k, cache_mem_v, state_ssm_re, state_ssm_im, state_rwkv, state_shift, page_table, mem_prompt, norm_mix, w_in, lam_q1, lam_k1, lam_q2, lam_k2, subln, proj_a, s5_lam_re, s5_lam_im, s5_log_dt, s5_b_re, s5_b_im, s5_c_re, s5_c_im, s5_d, s5_w_glu, s5_b_glu, proj_b, rw_mu, rw_w0, rw_w2, rw_a0, rw_a2, rw_g2, rw_k_k, rw_k_a, rw_r_k, rw_ln_w, rw_ln_b, proj_c, w_out, norm_x, norm_mem, w_xq, w_mk, w_mv, w_xo, norm_ffn, w_gate, w_up, w_down, norm_final):
    depth = w_in.shape[0]
    bp, seq, d = x_prompt.shape
    bs, dseq, _ = x_sample.shape
    assert bp == 1
    pool = cache_k.shape[1]
    ck = cache_k.transpose(0, 1, 3, 4, 5, 2).reshape(depth, pool, HA * 2 * DK, PAGE_SIZE)
    cv = cache_v.reshape(depth, pool, PAGE_SIZE * HA, DV)

    xp = x_prompt.reshape(bp * seq, d)
    xs = x_sample.reshape(bs * dseq, d)
    zeros_ssm = jnp.zeros((bp, S5_LANES), F32)
    zeros_rwkv = jnp.zeros((bp, HC, NC, NC), F32)
    zeros_shift = jnp.zeros((bp, 1, C_PROJ), F32)
    outs = [[] for _ in range(14)]
    for l in range(depth):
        raw = dict(lam_q1=lam_q1[l], lam_k1=lam_k1[l], lam_q2=lam_q2[l], lam_k2=lam_k2[l],
                   s5_lam_re=s5_lam_re[l], s5_lam_im=s5_lam_im[l], s5_log_dt=s5_log_dt[l],
                   s5_b_re=s5_b_re[l], s5_b_im=s5_b_im[l], s5_c_re=s5_c_re[l], s5_c_im=s5_c_im[l],
                   s5_d=s5_d[l], s5_w_glu=s5_w_glu[l], s5_b_glu=s5_b_glu[l],
                   rw_mu=rw_mu[l], rw_w0=rw_w0[l], rw_w2=rw_w2[l], rw_a0=rw_a0[l], rw_a2=rw_a2[l],
                   rw_g2=rw_g2[l], rw_k_k=rw_k_k[l], rw_k_a=rw_k_a[l], rw_r_k=rw_r_k[l].reshape(-1),
                   rw_ln_w=rw_ln_w[l], rw_ln_b=rw_ln_b[l])
        lp = dict(raw)
        lp.update(
            norm_mix=norm_mix[l][None, :], w_in_b=w_in[l].astype(BF16), subln=subln[l][None, :],
            proj_a=proj_a[l].astype(BF16), proj_b=proj_b[l].astype(BF16), proj_c=proj_c[l].astype(BF16),
            w_out=w_out[l].astype(BF16), norm_x=norm_x[l][None, :], w_xq=w_xq[l].astype(BF16),
            w_xo=w_xo[l].astype(BF16), norm_ffn=norm_ffn[l][None, :], w_gate=w_gate[l].astype(BF16),
            w_up=w_up[l].astype(BF16), w_down=w_down[l].astype(BF16),
            s5=_s5_params(raw), rw=_rwkv_params(raw))

        w_mkv_t = jnp.concatenate([w_mk[l], w_mv[l]], axis=1).T.astype(BF16)
        mkv_t = _mem_kv(mem_prompt.reshape(bp * N_MEM, d), norm_mem[l][None, :], w_mkv_t)
        mk_t = mkv_t[:X_WIDTH].reshape(bp, X_WIDTH, N_MEM)
        mv_t = mkv_t[X_WIDTH:].reshape(bp, X_WIDTH, N_MEM)

        def attn_p(lam, q_b, k_f, k_b, v_f, v_b, sub_g, post):
            return _attn_prompt(lam, q_b, k_b, v_b, sub_g, post)

        xp, kp, vp, srp, sip, rwp, shp = _layer(
            l, xp, bp, seq, attn_p, mk_t, mv_t, zeros_ssm, zeros_ssm, zeros_rwkv, zeros_shift, lp)

        def feature_major(mem):
            return mem.transpose(0, 2, 3, 1).reshape(mem.shape[0], X_WIDTH, N_MEM)

        def token_major(mem_t):
            return mem_t.reshape(bp, HX, DX, N_MEM).transpose(0, 3, 1, 2)

        def attn_s(lam, q_b, k_f, k_b, v_f, v_b, sub_g, post, _l=l):
            shp3 = (bs, dseq, HA * 2 * DK)
            return _attn_sample(_l, page_table, lam, q_b.astype(F32).reshape(shp3), k_f.reshape(shp3),
                                v_f.reshape(shp3), sub_g, ck, cv, post).reshape(bs * dseq, A_WIDTH)

        xs, ks, vs, srs, sis, rws, shs = _layer(
            l, xs, bs, dseq, attn_s, feature_major(cache_mem_k[l]), feature_major(cache_mem_v[l]),
            state_ssm_re[l].reshape(bs, S5_LANES), state_ssm_im[l].reshape(bs, S5_LANES),
            state_rwkv[l], state_shift[l].reshape(bs, 1, C_PROJ), lp)

        vals = [kp.reshape(bp, seq, HA, 2, DK), vp.reshape(bp, seq, HA, DV),
                token_major(mk_t), token_major(mv_t),
                srp.reshape(bp, S5_GROUPS, S5_STATE), sip.reshape(bp, S5_GROUPS, S5_STATE), rwp, shp,
                ks.reshape(bs, dseq, HA, 2, DK), vs.reshape(bs, dseq, HA, DV),
                srs.reshape(bs, S5_GROUPS, S5_STATE), sis.reshape(bs, S5_GROUPS, S5_STATE), rws, shs]
        for o, val in zip(outs, vals):
            o.append(val)

    y_prompt = _final_norm(xp, norm_final[None, :], min(ROW_TILE, bp * seq)).reshape(bp, seq, d)
    y_sample = _final_norm(xs, norm_final[None, :], min(ROW_TILE, bs * dseq)).reshape(bs, dseq, d)
    return (y_prompt, y_sample) + tuple(jnp.stack(o) for o in outs)
```

```python
import functools
import math

import jax
import jax.numpy as jnp
from jax import lax
from jax.experimental import pallas as pl
from jax.experimental.pallas import tpu as pltpu

F32 = jnp.float32
BF16 = jnp.bfloat16

D_MODEL = 1024
PAGE_SIZE = 128
HA, DK = 4, 64
DV = 2 * DK
A_WIDTH = HA * DV
S5_GROUP, S5_STATE, S5_WIDTH = 16, 64, 512
S5_GROUPS = S5_WIDTH // S5_GROUP
S5_LANES = S5_GROUPS * S5_STATE
HC, NC = 8, 64
C_WIDTH = HC * NC
DECAY_LORA, AAA_LORA, GATE_LORA = 64, 64, 128
C_PROJ = 3 * C_WIDTH + DECAY_LORA + AAA_LORA + GATE_LORA
N_MEM, HX, DX = 256, 4, 64
X_WIDTH = HX * DX
RMS_EPS = 1e-6
GN_EPS = 64e-5
NEG_INF = -1e30

V7X_SUBLANES = 8
V7X_VMEM_BYTES = 64 * 1024 * 1024
VMEM_LIMIT = V7X_VMEM_BYTES - 8 * 1024 * 1024

ROW_TILE = 256
ATT_TQ = 512
ATT_TK = 512
PAGES_PER_STEP = 8
S5_TILE = 256
S5_LANE_CHUNK = 512
RW_CHUNK = 64
RW_CHUNKS_PER_STEP = 4


def _params(*sem):
    return pltpu.CompilerParams(dimension_semantics=sem, vmem_limit_bytes=VMEM_LIMIT)


def _resident(shape):
    nd = len(shape)
    return pl.BlockSpec(shape, lambda *_: (0,) * nd, pipeline_mode=pl.Buffered(1))


def _rms(x, g):
    return x * lax.rsqrt(jnp.mean(x * x, axis=-1, keepdims=True) + RMS_EPS) * g


def _dot(a, b):
    return jnp.dot(a, b, preferred_element_type=F32)


def _dot_nt(a, b):
    return lax.dot_general(a, b, (((1,), (1,)), ((), ())), preferred_element_type=F32)


def _dot_tn(a, b):
    return lax.dot_general(a, b, (((0,), (0,)), ((), ())), preferred_element_type=F32)


def _split3(x):
    hi = x.astype(BF16)
    r1 = x - hi.astype(F32)
    mid = r1.astype(BF16)
    lo = (r1 - mid.astype(F32)).astype(BF16)
    return hi, mid, lo


def _dot_exact_rhs(sel, x):
    hi, mid, lo = _split3(x)
    return _dot(sel, hi) + _dot(sel, mid) + _dot(sel, lo)


def _dot_exact_lhs(x, sel):
    hi, mid, lo = _split3(x)
    return _dot(hi, sel) + _dot(mid, sel) + _dot(lo, sel)


def _norm_proj_body(splits, t_splits, x_ref, g_ref, w_ref, *rest):
    hb = _rms(x_ref[...], g_ref[...]).astype(BF16)
    out_refs = rest[1:] if t_splits else rest
    k = 0
    for off, width, outs in splits:
        y = _dot(hb, w_ref[:, off:off + width])
        for fn, _ in outs:
            out_refs[k][...] = fn(y).astype(out_refs[k].dtype)
            k += 1
    for off, width, fn, _ in t_splits:
        out_refs[k][...] = fn(_dot_nt(rest[0][off:off + width, :], hb)).astype(out_refs[k].dtype)
        k += 1


def _norm_proj(x, g, w, splits, tm, wt=None, t_splits=()):
    n, d = x.shape
    out_shape, out_specs = [], []
    for _, width, outs in splits:
        for _, dt in outs:
            out_shape.append(jax.ShapeDtypeStruct((n, width), dt))
            out_specs.append(pl.BlockSpec((tm, width), lambda i: (i, 0)))
    for _, width, _, dt in t_splits:
        out_shape.append(jax.ShapeDtypeStruct((width, n), dt))
        out_specs.append(pl.BlockSpec((width, tm), lambda i: (0, i)))
    extra = [wt] if t_splits else []
    return pl.pallas_call(
        functools.partial(_norm_proj_body, splits, tuple(t_splits)),
        grid=(n // tm,),
        in_specs=[pl.BlockSpec((tm, d), lambda i: (i, 0)), _resident(g.shape), _resident(w.shape)]
                 + [_resident(e.shape) for e in extra],
        out_specs=out_specs,
        out_shape=out_shape,
        compiler_params=_params("parallel"),
        name="norm_proj",
    )(x, g, w, *extra)


def _mem_kv_body(x_ref, g_ref, wt_ref, o_ref):
    hb = _rms(x_ref[...], g_ref[...]).astype(BF16)
    o_ref[...] = _dot_nt(wt_ref[...], hb)


def _mem_kv(mem, g, wt):
    return pl.pallas_call(
        _mem_kv_body,
        out_shape=jax.ShapeDtypeStruct((wt.shape[0], mem.shape[0]), F32),
        compiler_params=pltpu.CompilerParams(vmem_limit_bytes=VMEM_LIMIT),
        name="mem_kv",
    )(mem, g, wt)


def _ident(y):
    return y


def _scale_q(y):
    return y * (DK ** -0.5)


def _attn_prompt_body(post_scale, qi_ref, kj_ref, lam_ref, qt_ref, k_ref, vt_ref, g_ref, o_ref,
                      m_sc, l_sc, acc_sc):
    pair = pl.program_id(1)
    i = qi_ref[pair]
    j = kj_ref[pair]
    tq, tk = qt_ref.shape[1], k_ref.shape[0]

    @pl.when(j == 0)
    def _():
        m_sc[...] = jnp.full(m_sc.shape, NEG_INF, F32)
        l_sc[...] = jnp.zeros(l_sc.shape, F32)
        acc_sc[...] = jnp.zeros(acc_sc.shape, F32)

    def step(diagonal):
        qt = qt_ref[...]
        k = k_ref[...]
        vt = vt_ref[...]
        feat = lax.broadcasted_iota(jnp.int32, qt.shape, 0)
        zero = jnp.zeros_like(qt)
        st = [_dot(k, jnp.where(feat < DK, qt, zero)), _dot(k, jnp.where(feat >= DK, qt, zero))]
        if diagonal:
            key = lax.broadcasted_iota(jnp.int32, (tk, tq), 0)
            qry = lax.broadcasted_iota(jnp.int32, (tk, tq), 1)
            st = [jnp.where(key <= qry, s, NEG_INF) for s in st]
        m_prev = [m_sc[c] for c in range(2)]
        m_new = [jnp.maximum(m_prev[c], jnp.max(st[c], axis=0, keepdims=True)) for c in range(2)]
        alpha = [jnp.exp(m_prev[c] - m_new[c]) for c in range(2)]
        p = [jnp.exp(st[c] - m_new[c]) for c in range(2)]
        for c in range(2):
            l_sc[c] = alpha[c] * l_sc[c] + jnp.sum(p[c], axis=0, keepdims=True)
            acc_sc[c] = alpha[c] * acc_sc[c] + _dot(vt, p[c].astype(BF16))
            m_sc[c] = m_new[c]

    @pl.when(j < i)
    def _():
        step(False)

    @pl.when(j == i)
    def _():
        step(True)
        ot = acc_sc[0] / l_sc[0] - lam_ref[0] * (acc_sc[1] / l_sc[1])
        o_ref[...] = (_rms(ot.T, g_ref[...]) * post_scale).astype(o_ref.dtype)


def _attn_prompt(lam, qt, k, vt, subln, post_scale):
    n = k.shape[0]
    t = min(ATT_TQ, n)
    nq = n // t
    qi = jnp.asarray([i for i in range(nq) for _ in range(i + 1)], jnp.int32)
    kj = jnp.asarray([j for i in range(nq) for j in range(i + 1)], jnp.int32)
    grid_spec = pltpu.PrefetchScalarGridSpec(
        num_scalar_prefetch=2,
        grid=(HA, qi.shape[0]),
        in_specs=[
            pl.BlockSpec(memory_space=pltpu.SMEM),
            pl.BlockSpec((DV, t), lambda h, p, qi, kj: (h, qi[p])),
            pl.BlockSpec((t, DV), lambda h, p, qi, kj: (kj[p], h)),
            pl.BlockSpec((DV, t), lambda h, p, qi, kj: (h, kj[p])),
            pl.BlockSpec((1, DV), lambda h, p, qi, kj: (0, 0)),
        ],
        out_specs=pl.BlockSpec((t, DV), lambda h, p, qi, kj: (qi[p], h)),
        scratch_shapes=[pltpu.VMEM((2, 1, t), F32), pltpu.VMEM((2, 1, t), F32),
                        pltpu.VMEM((2, DV, t), F32)],
    )
    return pl.pallas_call(
        functools.partial(_attn_prompt_body, post_scale),
        grid_spec=grid_spec,
        out_shape=jax.ShapeDtypeStruct((n, A_WIDTH), BF16),
        compiler_params=_params("parallel", "arbitrary"),
        name="attn_prompt",
    )(qi, kj, lam, qt, k, vt, subln)


def _attn_sample_body(post_scale, n_tok, pt_ref, lam_ref, q_ref, kn_ref, vn_ref, g_ref, *rest):
    del pt_ref
    npg = PAGES_PER_STEP
    k_refs, v_refs = rest[:npg], rest[npg:2 * npg]
    o_ref, m_sc, l_sc, acc_sc = rest[2 * npg:]
    j = pl.program_id(1)
    hrows = 2 * n_tok
    nrow = HA * hrows
    width = HA * 2 * DK

    rid = lax.broadcasted_iota(jnp.int32, (nrow, width), 0)
    cid = lax.broadcasted_iota(jnp.int32, (nrow, width), 1)
    qrep = jnp.concatenate([q_ref[...]] * (2 * HA), axis=0)
    qblk = jnp.where(cid // DK == rid // n_tok, qrep, jnp.zeros_like(qrep)).astype(BF16)

    @pl.when(j == 0)
    def _():
        m_sc[...] = jnp.full(m_sc.shape, NEG_INF, F32)
        l_sc[...] = jnp.zeros(l_sc.shape, F32)
        acc_sc[...] = jnp.zeros(acc_sc.shape, F32)

    def update(s_pages, v_pages):
        m_prev = m_sc[...]
        m_new = m_prev
        for s in s_pages:
            m_new = jnp.maximum(m_new, jnp.max(s, axis=-1, keepdims=True))
        alpha = jnp.exp(m_prev - m_new)
        l_new = alpha * l_sc[...]
        pv = [None] * HA
        for s, v_heads in zip(s_pages, v_pages):
            p = jnp.exp(s - m_new)
            l_new = l_new + jnp.sum(p, axis=-1, keepdims=True)
            pb = p.astype(BF16)
            for h in range(HA):
                d = _dot(pb[h * hrows:(h + 1) * hrows], v_heads[h])
                pv[h] = d if pv[h] is None else pv[h] + d
        l_sc[...] = l_new
        acc_sc[...] = alpha * acc_sc[...] + jnp.concatenate(pv, axis=0)
        m_sc[...] = m_new

    update([_dot(qblk, k_refs[i][...].astype(BF16)) for i in range(npg)],
           [[v_refs[i][pl.ds(h, PAGE_SIZE, stride=HA), :].astype(BF16) for h in range(HA)]
            for i in range(npg)])

    @pl.when(j == pl.num_programs(1) - 1)
    def _():
        pad = jnp.zeros((PAGE_SIZE - n_tok, width), F32)
        kn = jnp.concatenate([kn_ref[...], pad], axis=0).astype(BF16)
        vn = jnp.concatenate([vn_ref[...], pad], axis=0).astype(BF16)
        s = _dot_nt(qblk, kn)
        r2 = lax.broadcasted_iota(jnp.int32, (nrow, PAGE_SIZE), 0) % n_tok
        c2 = lax.broadcasted_iota(jnp.int32, (nrow, PAGE_SIZE), 1)
        s = jnp.where(c2 <= r2, s, NEG_INF)
        update([s], [[vn[:, h * DV:(h + 1) * DV] for h in range(HA)]])
        sel = acc_sc[...] / l_sc[...]
        lam = lam_ref[0]
        heads = []
        for h in range(HA):
            o = sel[h * hrows:h * hrows + n_tok] - lam * sel[h * hrows + n_tok:(h + 1) * hrows]
            heads.append(_rms(o, g_ref[...]) * post_scale)
        o_ref[...] = jnp.concatenate(heads, axis=1).astype(o_ref.dtype)


def _attn_sample(layer, page_table, lam, q, k_new, v_new, subln, cache_kt, cache_vr, post_scale):
    b, n_tok, width = q.shape
    n_pages = page_table.shape[1]
    npg = PAGES_PER_STEP
    nj = n_pages // npg
    pt = page_table.reshape(-1)

    def page_spec(i):
        return pl.BlockSpec((None, None, width, PAGE_SIZE),
                            lambda bb, j, p: (layer, p[bb * n_pages + j * npg + i], 0, 0))

    tok_spec = pl.BlockSpec((None, n_tok, width), lambda bb, j, p: (bb, 0, 0))
    nrow = 2 * HA * n_tok
    grid_spec = pltpu.PrefetchScalarGridSpec(
        num_scalar_prefetch=1,
        grid=(b, nj),
        in_specs=[pl.BlockSpec(memory_space=pltpu.SMEM), tok_spec, tok_spec, tok_spec,
                  pl.BlockSpec((1, DV), lambda bb, j, p: (0, 0))]
                 + [page_spec(i) for i in range(npg)] + [page_spec(i) for i in range(npg)],
        out_specs=tok_spec,
        scratch_shapes=[pltpu.VMEM((nrow, 1), F32), pltpu.VMEM((nrow, 1), F32),
                        pltpu.VMEM((nrow, DV), F32)],
    )
    return pl.pallas_call(
        functools.partial(_attn_sample_body, post_scale, n_tok),
        grid_spec=grid_spec,
        out_shape=jax.ShapeDtypeStruct((b, n_tok, width), F32),
        compiler_params=_params("parallel", "arbitrary"),
        name="attn_sample",
    )(pt, lam, q, k_new, v_new, subln, *([cache_kt] * npg), *([cache_vr] * npg))


def _s5_body(chained, u_ref, x0r_ref, x0i_ref, tab_ref, bre_ref, bim_ref, cre_ref, cim_ref,
             d_ref, wg_ref, bg_ref, y_ref, xfr_ref, xfi_ref, sr_sc, si_sc, st_sc):
    i = pl.program_id(0)
    tt = u_ref.shape[0]
    sub = V7X_SUBLANES
    u = u_ref[...]
    ub = u.astype(BF16)
    sr_sc[sub:, :] = _dot(ub, bre_ref[...])
    si_sc[sub:, :] = _dot(ub, bim_ref[...])

    if chained:
        @pl.when(i == 0)
        def _():
            st_sc[0:1, :] = x0r_ref[...]
            st_sc[1:2, :] = x0i_ref[...]

        sr_sc[sub - 1:sub, :] = st_sc[0:1, :]
        si_sc[sub - 1:sub, :] = st_sc[1:2, :]

    def group(gi, carry):
        r0 = pl.multiple_of(gi * sub, sub)
        for lc in range(S5_LANES // S5_LANE_CHUNK):
            ls = slice(lc * S5_LANE_CHUNK, (lc + 1) * S5_LANE_CHUNK)
            vr = sr_sc[pl.ds(r0 + sub, sub), ls]
            vi = si_sc[pl.ds(r0 + sub, sub), ls]
            for t, d in enumerate((1, 2, 4)):
                ar, ai = tab_ref[2 * t, :, ls], tab_ref[2 * t + 1, :, ls]
                pr, pi = pltpu.roll(vr, d, axis=0), pltpu.roll(vi, d, axis=0)
                vr, vi = vr + ar * pr - ai * pi, vi + ar * pi + ai * pr
            if chained:
                cr = sr_sc[pl.ds(r0 + sub - 1, 1), ls]
                ci = si_sc[pl.ds(r0 + sub - 1, 1), ls]
            else:
                cr = x0r_ref[pl.ds(gi, 1), ls]
                ci = x0i_ref[pl.ds(gi, 1), ls]
            cr = jnp.broadcast_to(cr, vr.shape)
            ci = jnp.broadcast_to(ci, vi.shape)
            pr, pi = tab_ref[6, :, ls], tab_ref[7, :, ls]
            vr, vi = vr + pr * cr - pi * ci, vi + pr * ci + pi * cr
            sr_sc[pl.ds(r0 + sub, sub), ls] = vr
            si_sc[pl.ds(r0 + sub, sub), ls] = vi
            if not chained:
                xfr_ref[pl.ds(gi, 1), ls] = vr[sub - 1:sub]
                xfi_ref[pl.ds(gi, 1), ls] = vi[sub - 1:sub]
        return carry

    lax.fori_loop(0, tt // sub, group, 0)

    if chained:
        st_sc[0:1, :] = sr_sc[tt + sub - 1:tt + sub, :]
        st_sc[1:2, :] = si_sc[tt + sub - 1:tt + sub, :]
        xfr_ref[...] = st_sc[0:1, :]
        xfi_ref[...] = st_sc[1:2, :]

    xr = sr_sc[sub:, :].astype(BF16)
    xi = si_sc[sub:, :].astype(BF16)
    y = _dot(xr, cre_ref[...]) - _dot(xi, cim_ref[...]) + d_ref[...] * u
    z = y * (0.5 * (1.0 + jnp.tanh(math.sqrt(2.0 / math.pi) * (y + 0.044715 * (y * y * y)))))
    gate = jax.nn.sigmoid(_dot(z.astype(BF16), wg_ref[...]) + bg_ref[...])
    y_ref[...] = (z * gate).astype(y_ref.dtype)


def _s5(u, x0r, x0i, sp, chained):
    n = u.shape[0]
    tt = min(S5_TILE, n)
    ng = x0r.shape[0]
    if chained:
        st_spec = pl.BlockSpec((1, S5_LANES), lambda i: (0, 0))
    else:
        assert tt == n and ng * V7X_SUBLANES == n
        st_spec = pl.BlockSpec((ng, S5_LANES), lambda i: (0, 0))
    consts = [sp["tab"], sp["b_re"], sp["b_im"], sp["c_re"], sp["c_im"], sp["d"], sp["w_glu"], sp["b_glu"]]
    return pl.pallas_call(
        functools.partial(_s5_body, chained),
        grid=(n // tt,),
        in_specs=[pl.BlockSpec((tt, S5_WIDTH), lambda i: (i, 0)), st_spec, st_spec]
                 + [_resident(c.shape) for c in consts],
        out_specs=[pl.BlockSpec((tt, S5_WIDTH), lambda i: (i, 0)), st_spec, st_spec],
        out_shape=[jax.ShapeDtypeStruct((n, S5_WIDTH), BF16),
                   jax.ShapeDtypeStruct((ng, S5_LANES), F32),
                   jax.ShapeDtypeStruct((ng, S5_LANES), F32)],
        scratch_shapes=[pltpu.VMEM((tt + V7X_SUBLANES, S5_LANES), F32),
                        pltpu.VMEM((tt + V7X_SUBLANES, S5_LANES), F32),
                        pltpu.VMEM((2, S5_LANES), F32)],
        compiler_params=_params("arbitrary"),
        name="s5_branch",
    )(u, x0r, x0i, *consts)


def _rwkv_body(t_valid, pc_ref, prev_ref, sh_ref, s0_ref, mu_ref, w0_ref, w2_ref, a0_ref, a2_ref,
               g2_ref, kk_ref, ka_ref, rk_ref, lnw_ref, lnb_ref, ones_ref,
               y_ref, sf_ref, s_sc):
    j = pl.program_id(1)
    t = RW_CHUNK
    rows_in = pc_ref.shape[0]
    tp = max(rows_in, t)
    nch = tp // t

    @pl.when(j == 0)
    def _():
        s_sc[...] = s0_ref[...]

    pc = pc_ref[...]
    if rows_in < tp:
        pc = jnp.concatenate([pc, jnp.zeros((tp - rows_in, C_PROJ), F32)], axis=0)
    prev_row = jnp.where(j == 0, sh_ref[...], prev_ref[V7X_SUBLANES - 1:V7X_SUBLANES, :])
    row_id = lax.broadcasted_iota(jnp.int32, (tp, C_PROJ), 0)
    shifted = jnp.where(row_id == 0, jnp.broadcast_to(prev_row, pc.shape), pltpu.roll(pc, 1, axis=0))
    xm = pc + (shifted - pc) * mu_ref[...]

    c = C_WIDTH
    r = xm[:, 0:c]
    k = xm[:, c:2 * c]
    v = xm[:, 2 * c:3 * c]
    pw = xm[:, 3 * c:3 * c + DECAY_LORA]
    pa = xm[:, 3 * c + DECAY_LORA:3 * c + DECAY_LORA + AAA_LORA]
    pg = xm[:, 3 * c + DECAY_LORA + AAA_LORA:]

    wl = w0_ref[...] + _dot(jnp.tanh(pw).astype(BF16), w2_ref[...])
    neg = -wl
    softplus = jnp.maximum(neg, 0.0) + jnp.log1p(jnp.exp(-jnp.abs(neg)))
    logw = -jnp.exp(-softplus - 0.5)
    a = jax.nn.sigmoid(a0_ref[...] + _dot(pa.astype(BF16), a2_ref[...]))
    g = _dot(jax.nn.sigmoid(pg).astype(BF16), g2_ref[...])

    ones = ones_ref[...]
    kk = k * kk_ref[...]
    kk = kk / jnp.maximum(jnp.sqrt(_dot_exact_lhs(kk * kk, ones)), 1e-12)
    kh = k * (1.0 + (a - 1.0) * ka_ref[...])
    alpha = -kk
    beta = kk * a

    if t_valid < tp:
        valid = lax.broadcasted_iota(jnp.int32, (tp, c), 0) < t_valid
        zero = jnp.zeros((tp, c), F32)
        logw = jnp.where(valid, logw, zero)
        alpha = jnp.where(valid, alpha, zero)
        beta = jnp.where(valid, beta, zero)
        kh_m = jnp.where(valid, kh, zero)
        v_m = jnp.where(valid, v, zero)
    else:
        kh_m, v_m = kh, v

    ri = lax.broadcasted_iota(jnp.int32, (tp, tp), 0)
    ci = lax.broadcasted_iota(jnp.int32, (tp, tp), 1)
    blocktril = jnp.logical_and(ri // t == ci // t, ci <= ri).astype(BF16)
    cum = _dot_exact_rhs(blocktril, logw)
    c_end = jnp.concatenate(
        [jnp.broadcast_to(cum[(q + 1) * t - 1:(q + 1) * t, :], (t, c)) for q in range(nch)], axis=0)
    e_cum = jnp.exp(cum)
    e_neg = jnp.exp(-cum)
    e_tail = jnp.exp(c_end - cum)
    at = (alpha * jnp.exp(cum - logw)).astype(BF16)
    rt = r * e_cum
    rtb = rt.astype(BF16)
    bt = (beta * e_neg).astype(BF16)
    kt = (kh_m * e_neg).astype(BF16)
    bh = (beta * e_tail).astype(BF16)
    kx = (kh_m * e_tail).astype(BF16)
    vb = v_m.astype(BF16)
    w_end = jnp.exp(c_end)

    ti = lax.broadcasted_iota(jnp.int32, (t, t), 0)
    si = lax.broadcasted_iota(jnp.int32, (t, t), 1)
    incl = si <= ti
    strict = si < ti
    eye = (si == ti).astype(F32)

    pairs = [(q, h) for q in range(nch) for h in range(HC)]

    def blk(x, q, h):
        return x[q * t:(q + 1) * t, h * NC:(h + 1) * NC]

    def stage(fn):
        return [fn(q, h, i) for i, (q, h) in enumerate(pairs)]

    ar = stage(lambda q, h, i: jnp.concatenate([blk(at, q, h), blk(rtb, q, h)], axis=0))
    xb = stage(lambda q, h, i: _dot_nt(ar[i], blk(bt, q, h)))
    xk = stage(lambda q, h, i: _dot_nt(ar[i], blk(kt, q, h)))
    n_ab = stage(lambda q, h, i: jnp.where(strict, xb[i][:t], 0.0))
    a_br = stage(lambda q, h, i: jnp.where(incl, xb[i][t:], 0.0).astype(BF16))
    a_ak = stage(lambda q, h, i: jnp.where(strict, xk[i][:t], 0.0).astype(BF16))
    a_kr = stage(lambda q, h, i: jnp.where(incl, xk[i][t:], 0.0).astype(BF16))
    minv = stage(lambda q, h, i: eye + n_ab[i])
    pw2 = n_ab
    span = 1
    while span * 2 < t:
        pb = [x.astype(BF16) for x in pw2]
        pw2 = stage(lambda q, h, i: _dot(pb[i], pb[i]))
        minv = stage(lambda q, h, i: minv[i] + _dot(pw2[i].astype(BF16), minv[i].astype(BF16)))
        span *= 2
    minvb = [x.astype(BF16) for x in minv]
    akv = stage(lambda q, h, i: _dot(a_ak[i], blk(vb, q, h)).astype(BF16))
    p1 = stage(lambda q, h, i: _dot(minvb[i], blk(at, q, h)).astype(BF16))
    p2 = stage(lambda q, h, i: _dot(minvb[i], akv[i]).astype(BF16))
    qm = stage(lambda q, h, i: (blk(rt, q, h) + _dot(a_br[i], p1[i])).astype(BF16))
    zm = stage(lambda q, h, i: _dot(a_br[i], p2[i]) + _dot(a_kr[i], blk(vb, q, h)))
    gm = stage(lambda q, h, i: _dot_tn(p1[i], blk(bh, q, h)).astype(BF16))
    hm = stage(lambda q, h, i: _dot_tn(p2[i], blk(bh, q, h)) + _dot_tn(blk(vb, q, h), blk(kx, q, h)))

    s_cur = [s_sc[h] for h in range(HC)]
    y_rows = []
    for q in range(nch):
        ys = []
        for h in range(HC):
            i = q * HC + h
            sb = s_cur[h].astype(BF16)
            ys.append(_dot_nt(qm[i], sb) + zm[i])
            s_cur[h] = s_cur[h] * blk(w_end, q, h)[0:1, :] + _dot(sb, gm[i]) + hm[i]
        y_rows.append(jnp.concatenate(ys, axis=1))
    for h in range(HC):
        s_sc[h] = s_cur[h]
        sf_ref[h] = s_cur[h]

    y = y_rows[0] if nch == 1 else jnp.concatenate(y_rows, axis=0)
    inv_n = 1.0 / NC
    mean = _dot_exact_lhs(y, ones) * inv_n
    dlt = y - mean
    var = _dot_exact_lhs(dlt * dlt, ones) * inv_n
    yn = dlt * lax.rsqrt(var + GN_EPS) * lnw_ref[...] + lnb_ref[...]
    bonus = _dot_exact_lhs(r * kh * rk_ref[...], ones) * v
    out = (yn + bonus) * g
    y_ref[...] = out[:rows_in].astype(y_ref.dtype)


def _rwkv(pc, shift_prev, s0, rp, batch, seq):
    rows = min(RW_CHUNK * RW_CHUNKS_PER_STEP, seq)
    nt = seq // rows
    sub = V7X_SUBLANES
    per8 = seq // sub
    consts = [rp["mu"], rp["w0"], rp["w2"], rp["a0"], rp["a2"], rp["g2"], rp["k_k"], rp["k_a"],
              rp["r_k"], rp["ln_w"], rp["ln_b"], rp["ones"]]
    return pl.pallas_call(
        functools.partial(_rwkv_body, rows),
        grid=(batch, nt),
        in_specs=[
            pl.BlockSpec((rows, C_PROJ), lambda b, j: (b * nt + j, 0)),
            pl.BlockSpec((sub, C_PROJ), lambda b, j: (b * per8 + jnp.maximum(j * (rows // sub) - 1, 0), 0)),
            pl.BlockSpec((None, 1, C_PROJ), lambda b, j: (b, 0, 0)),
            pl.BlockSpec((None, HC, NC, NC), lambda b, j: (b, 0, 0, 0)),
        ] + [_resident(c.shape) for c in consts],
        out_specs=[pl.BlockSpec((rows, C_WIDTH), lambda b, j: (b * nt + j, 0)),
                   pl.BlockSpec((None, HC, NC, NC), lambda b, j: (b, 0, 0, 0))],
        out_shape=[jax.ShapeDtypeStruct((batch * seq, C_WIDTH), F32),
                   jax.ShapeDtypeStruct((batch, HC, NC, NC), F32)],
        scratch_shapes=[pltpu.VMEM((HC, NC, NC), F32)],
        compiler_params=_params("parallel", "arbitrary"),
        name="rwkv_branch",
    )(pc, pc, shift_prev, s0, *consts)


def _merge_body(x_ref, sg_ref, ya_ref, yb_ref, yc_ref, pa_ref, pb_ref, pc_ref, wo_ref, o_ref):
    d = D_MODEL
    sg = sg_ref[...]
    merged = (sg[:, 0:d] * _dot(ya_ref[...].astype(BF16), pa_ref[...])
              + sg[:, d:2 * d] * _dot(yb_ref[...].astype(BF16), pb_ref[...])
              + sg[:, 2 * d:3 * d] * _dot(yc_ref[...].astype(BF16), pc_ref[...]))
    o_ref[...] = x_ref[...] + _dot(merged.astype(BF16), wo_ref[...])


def _merge(x, sg, ya, yb, yc, wp, tm):
    n = x.shape[0]
    rows = lambda w: pl.BlockSpec((tm, w), lambda i: (i, 0))
    consts = [wp["proj_a"], wp["proj_b"], wp["proj_c"], wp["w_out"]]
    return pl.pallas_call(
        _merge_body,
        grid=(n // tm,),
        in_specs=[rows(D_MODEL), rows(3 * D_MODEL), rows(A_WIDTH), rows(S5_WIDTH), rows(C_WIDTH)]
                 + [_resident(c.shape) for c in consts],
        out_specs=rows(D_MODEL),
        out_shape=jax.ShapeDtypeStruct((n, D_MODEL), F32),
        compiler_params=_params("parallel"),
        name="merge_out",
    )(x, sg, ya, yb, yc, *consts)


def _cross_body(x_ref, g_ref, wq_ref, mk_ref, mv_ref, wo_ref, o_ref):
    x = x_ref[...]
    q = _dot(_rms(x, g_ref[...]).astype(BF16), wq_ref[...])
    mk = mk_ref[...].astype(BF16)
    mv = mv_ref[...].astype(BF16)
    heads = []
    for h in range(HX):
        hs = slice(h * DX, (h + 1) * DX)
        s = _dot(q[:, hs].astype(BF16), mk[hs, :]) * (DX ** -0.5)
        e = jnp.exp(s - jnp.max(s, axis=-1, keepdims=True))
        p = e / jnp.sum(e, axis=-1, keepdims=True)
        heads.append(_dot_nt(p.astype(BF16), mv[hs, :]))
    o = jnp.concatenate(heads, axis=1)
    o_ref[...] = x + _dot(o.astype(BF16), wo_ref[...])


def _cross(x, mem_k, mem_v, wp, batch, seq, tm):
    nt = seq // tm
    xs = pl.BlockSpec((tm, D_MODEL), lambda b, j: (b * nt + j, 0))
    ms = pl.BlockSpec((None, X_WIDTH, N_MEM), lambda b, j: (b, 0, 0))
    consts_a = [wp["norm_x"], wp["w_xq"]]
    return pl.pallas_call(
        _cross_body,
        grid=(batch, nt),
        in_specs=[xs] + [_resident(c.shape) for c in consts_a] + [ms, ms, _resident(wp["w_xo"].shape)],
        out_specs=xs,
        out_shape=jax.ShapeDtypeStruct((batch * seq, D_MODEL), F32),
        compiler_params=_params("parallel", "parallel"),
        name="cross_attn",
    )(x, *consts_a, mem_k, mem_v, wp["w_xo"])


def _ffn_body(x_ref, g_ref, wg_ref, wu_ref, wd_ref, o_ref):
    x = x_ref[...]
    hb = _rms(x, g_ref[...]).astype(BF16)
    gate = _dot(hb, wg_ref[...])
    up = _dot(hb, wu_ref[...])
    act = (gate * jax.nn.sigmoid(gate) * up).astype(BF16)
    o_ref[...] = x + _dot(act, wd_ref[...])


def _ffn(x, wp, tm):
    n = x.shape[0]
    rows = pl.BlockSpec((tm, D_MODEL), lambda i: (i, 0))
    consts = [wp["norm_ffn"], wp["w_gate"], wp["w_up"], wp["w_down"]]
    return pl.pallas_call(
        _ffn_body,
        grid=(n // tm,),
        in_specs=[rows] + [_resident(c.shape) for c in consts],
        out_specs=rows,
        out_shape=jax.ShapeDtypeStruct((n, D_MODEL), F32),
        compiler_params=_params("parallel"),
        name="swiglu",
    )(x, *consts)


def _final_norm_body(x_ref, g_ref, o_ref):
    o_ref[...] = _rms(x_ref[...], g_ref[...])


def _final_norm(x, g, tm):
    n = x.shape[0]
    rows = pl.BlockSpec((tm, D_MODEL), lambda i: (i, 0))
    return pl.pallas_call(
        _final_norm_body,
        grid=(n // tm,),
        in_specs=[rows, _resident(g.shape)],
        out_specs=rows,
        out_shape=jax.ShapeDtypeStruct((n, D_MODEL), F32),
        compiler_params=_params("parallel"),
        name="final_norm",
    )(x, g)


def _s5_params(lp):
    lam_re, lam_im = lp["s5_lam_re"], lp["s5_lam_im"]
    dt = jnp.exp(lp["s5_log_dt"])[:, None]
    mag = jnp.exp(lam_re * dt)
    ab_re = mag * jnp.cos(lam_im * dt)
    ab_im = mag * jnp.sin(lam_im * dt)
    den = lam_re * lam_re + lam_im * lam_im
    nr = ab_re - 1.0
    cr = (nr * lam_re + ab_im * lam_im) / den
    ci = (ab_im * lam_re - nr * lam_im) / den
    bb_re = cr[..., None] * lp["s5_b_re"] - ci[..., None] * lp["s5_b_im"]
    bb_im = cr[..., None] * lp["s5_b_im"] + ci[..., None] * lp["s5_b_re"]
    eye = jnp.eye(S5_GROUPS, dtype=F32)

    def blockdiag_in(b):
        return jnp.einsum("gph,gk->ghkp", b, eye).reshape(S5_WIDTH, S5_LANES)

    def blockdiag_out(cm):
        return jnp.einsum("ghp,gk->gpkh", cm, eye).reshape(S5_LANES, S5_WIDTH)

    ar, ai = ab_re.reshape(-1), ab_im.reshape(-1)
    pows_r, pows_i = [ar], [ai]
    for _ in range(V7X_SUBLANES - 1):
        pr, pi = pows_r[-1], pows_i[-1]
        pows_r.append(pr * ar - pi * ai)
        pows_i.append(pr * ai + pi * ar)
    rows = jnp.arange(V7X_SUBLANES)[:, None]
    tabs = []
    for d in (1, 2, 4):
        tabs.append(jnp.where(rows >= d, pows_r[d - 1][None, :], 0.0))
        tabs.append(jnp.where(rows >= d, pows_i[d - 1][None, :], 0.0))
    tabs.append(jnp.stack(pows_r))
    tabs.append(jnp.stack(pows_i))
    return {
        "tab": jnp.stack(tabs).astype(F32),
        "b_re": blockdiag_in(bb_re).astype(BF16), "b_im": blockdiag_in(bb_im).astype(BF16),
        "c_re": blockdiag_out(lp["s5_c_re"]).astype(BF16), "c_im": blockdiag_out(lp["s5_c_im"]).astype(BF16),
        "d": lp["s5_d"][None, :], "w_glu": lp["s5_w_glu"].astype(BF16), "b_glu": lp["s5_b_glu"][None, :],
    }


def _rwkv_params(lp):
    seg = jnp.arange(C_WIDTH) // NC
    row = lambda a: a.reshape(1, -1)
    return {
        "mu": row(lp["rw_mu"]), "w0": row(lp["rw_w0"]), "w2": lp["rw_w2"].astype(BF16),
        "a0": row(lp["rw_a0"]), "a2": lp["rw_a2"].astype(BF16), "g2": lp["rw_g2"].astype(BF16),
        "k_k": row(lp["rw_k_k"]), "k_a": row(lp["rw_k_a"]), "r_k": row(lp["rw_r_k"]),
        "ln_w": row(lp["rw_ln_w"]), "ln_b": row(lp["rw_ln_b"]),
        "ones": (seg[:, None] == seg[None, :]).astype(BF16),
    }


def _in_proj(x, lp, tm, feature_major_qv):
    qa = HA * 2 * DK
    o_u = 3 * qa
    o_pc = o_u + S5_WIDTH
    o_g = o_pc + C_PROJ
    tail = [
        (o_u, S5_WIDTH, [(_ident, F32)]),
        (o_pc, C_PROJ, [(_ident, F32)]),
        (o_g, 3 * D_MODEL, [(jax.nn.sigmoid, F32)]),
    ]
    if feature_major_qv:
        splits = [(qa, qa, [(_ident, F32), (_ident, BF16)]), (2 * qa, qa, [(_ident, F32)])] + tail
        t_splits = [(0, qa, _scale_q, BF16), (qa, qa, _ident, BF16)]
        k_f, k_b, v_f, u, pc, sg, q_t, v_t = _norm_proj(x, lp["norm_mix"], lp["w_in_b"], splits, tm,
                                                       wt=lp["w_qv_t"], t_splits=t_splits)
        return q_t, k_f, k_b, v_f, v_t, u, pc, sg
    splits = [(0, qa, [(_scale_q, F32)]), (qa, qa, [(_ident, F32)]), (2 * qa, qa, [(_ident, F32)])] + tail
    q, k_f, v_f, u, pc, sg = _norm_proj(x, lp["norm_mix"], lp["w_in_b"], splits, tm)
    return q, k_f, None, v_f, None, u, pc, sg


def _layer(l, x, batch, seq, attn_fn, mem_k, mem_v, ssm_re0, ssm_im0, rwkv0, shift0, lp):
    n = batch * seq
    tm = min(ROW_TILE, n)
    lam_init = 0.8 - 0.6 * math.exp(-0.3 * l)
    lam = (jnp.exp(jnp.sum(lp["lam_q1"] * lp["lam_k1"])) - jnp.exp(jnp.sum(lp["lam_q2"] * lp["lam_k2"]))
           + lam_init).reshape(1).astype(F32)
    q, k_f, k_b, v_f, v_t, u, pc, sg = _in_proj(x, lp, tm, feature_major_qv=(batch == 1))
    ya = attn_fn(lam, q, k_f, k_b, v_f, v_t, lp["subln"], 1.0 - lam_init)
    yb, ssm_re, ssm_im = _s5(u, ssm_re0, ssm_im0, lp["s5"], chained=(batch == 1))
    yc, rwkv_s = _rwkv(pc, shift0, rwkv0, lp["rw"], batch, seq)
    shift = pc.reshape(batch, seq, C_PROJ)[:, -1]
    x = _merge(x, sg, ya, yb, yc, lp, tm)
    x = _cross(x, mem_k, mem_v, lp, batch, seq, min(tm, seq))
    x = _ffn(x, lp, tm)
    return x, k_f, v_f, ssm_re, ssm_im, rwkv_s, shift


def kernel(x_prompt, x_sample, cache_k, cache_v, cache_mem_k, cache_mem_v, state_ssm_re, state_ssm_im, state_rwkv, state_shift, page_table, mem_prompt, norm_mix, w_in, lam_q1, lam_k1, lam_q2, lam_k2, subln, proj_a, s5_lam_re, s5_lam_im, s5_log_dt, s5_b_re, s5_b_im, s5_c_re, s5_c_im, s5_d, s5_w_glu, s5_b_glu, proj_b, rw_mu, rw_w0, rw_w2, rw_a0, rw_a2, rw_g2, rw_k_k, rw_k_a, rw_r_k, rw_ln_w, rw_ln_b, proj_c, w_out, norm_x, norm_mem, w_xq, w_mk, w_mv, w_xo, norm_ffn, w_gate, w_up, w_down, norm_final):
    depth = w_in.shape[0]
    bp, seq, d = x_prompt.shape
    bs, dseq, _ = x_sample.shape
    assert bp == 1
    pool = cache_k.shape[1]
    ck = cache_k.transpose(0, 1, 3, 4, 5, 2).reshape(depth, pool, HA * 2 * DK, PAGE_SIZE)
    cv = cache_v.reshape(depth, pool, PAGE_SIZE * HA, DV)

    xp = x_prompt.reshape(bp * seq, d)
    xs = x_sample.reshape(bs * dseq, d)
    zeros_ssm = jnp.zeros((bp, S5_LANES), F32)
    zeros_rwkv = jnp.zeros((bp, HC, NC, NC), F32)
    zeros_shift = jnp.zeros((bp, 1, C_PROJ), F32)
    outs = [[] for _ in range(14)]
    for l in range(depth):
        raw = dict(lam_q1=lam_q1[l], lam_k1=lam_k1[l], lam_q2=lam_q2[l], lam_k2=lam_k2[l],
                   s5_lam_re=s5_lam_re[l], s5_lam_im=s5_lam_im[l], s5_log_dt=s5_log_dt[l],
                   s5_b_re=s5_b_re[l], s5_b_im=s5_b_im[l], s5_c_re=s5_c_re[l], s5_c_im=s5_c_im[l],
                   s5_d=s5_d[l], s5_w_glu=s5_w_glu[l], s5_b_glu=s5_b_glu[l],
                   rw_mu=rw_mu[l], rw_w0=rw_w0[l], rw_w2=rw_w2[l], rw_a0=rw_a0[l], rw_a2=rw_a2[l],
                   rw_g2=rw_g2[l], rw_k_k=rw_k_k[l], rw_k_a=rw_k_a[l], rw_r_k=rw_r_k[l].reshape(-1),
                   rw_ln_w=rw_ln_w[l], rw_ln_b=rw_ln_b[l])
        lp = dict(raw)
        lp.update(
            norm_mix=norm_mix[l][None, :], w_in_b=w_in[l].astype(BF16), subln=subln[l][None, :],
            w_qv_t=jnp.concatenate([w_in[l][:, :HA * 2 * DK], w_in[l][:, 2 * HA * 2 * DK:3 * HA * 2 * DK]],
                                   axis=1).T.astype(BF16),
            proj_a=proj_a[l].astype(BF16), proj_b=proj_b[l].astype(BF16), proj_c=proj_c[l].astype(BF16),
            w_out=w_out[l].astype(BF16), norm_x=norm_x[l][None, :], w_xq=w_xq[l].astype(BF16),
            w_xo=w_xo[l].astype(BF16), norm_ffn=norm_ffn[l][None, :], w_gate=w_gate[l].astype(BF16),
            w_up=w_up[l].astype(BF16), w_down=w_down[l].astype(BF16),
            s5=_s5_params(raw), rw=_rwkv_params(raw))

        w_mkv_t = jnp.concatenate([w_mk[l], w_mv[l]], axis=1).T.astype(BF16)
        mkv_t = _mem_kv(mem_prompt.reshape(bp * N_MEM, d), norm_mem[l][None, :], w_mkv_t)
        mk_t = mkv_t[:X_WIDTH].reshape(bp, X_WIDTH, N_MEM)
        mv_t = mkv_t[X_WIDTH:].reshape(bp, X_WIDTH, N_MEM)

        def attn_p(lam, q_t, k_f, k_b, v_f, v_t, sub_g, post):
            return _attn_prompt(lam, q_t, k_b, v_t, sub_g, post)

        xp, kp, vp, srp, sip, rwp, shp = _layer(
            l, xp, bp, seq, attn_p, mk_t, mv_t, zeros_ssm, zeros_ssm, zeros_rwkv, zeros_shift, lp)

        def feature_major(mem):
            return mem.transpose(0, 2, 3, 1).reshape(mem.shape[0], X_WIDTH, N_MEM)

        def token_major(mem_t):
            return mem_t.reshape(bp, HX, DX, N_MEM).transpose(0, 3, 1, 2)

        def attn_s(lam, q, k_f, k_b, v_f, v_t, sub_g, post, _l=l):
            shp3 = (bs, dseq, HA * 2 * DK)
            return _attn_sample(_l, page_table, lam, q.reshape(shp3), k_f.reshape(shp3),
                                v_f.reshape(shp3), sub_g, ck, cv, post).reshape(bs * dseq, A_WIDTH)

        xs, ks, vs, srs, sis, rws, shs = _layer(
            l, xs, bs, dseq, attn_s, feature_major(cache_mem_k[l]), feature_major(cache_mem_v[l]),
            state_ssm_re[l].reshape(bs, S5_LANES), state_ssm_im[l].reshape(bs, S5_LANES),
            state_rwkv[l], state_shift[l].reshape(bs, 1, C_PROJ), lp)

        vals = [kp.reshape(bp, seq, HA, 2, DK), vp.reshape(bp, seq, HA, DV),
                token_major(mk_t), token_major(mv_t),
                srp.reshape(bp, S5_GROUPS, S5_STATE), sip.reshape(bp, S5_GROUPS, S5_STATE), rwp, shp,
                ks.reshape(bs, dseq, HA, 2, DK), vs.reshape(bs, dseq, HA, DV),
                srs.reshape(bs, S5_GROUPS, S5_STATE), sis.reshape(bs, S5_GROUPS, S5_STATE), rws, shs]
        for o, val in zip(outs, vals):
            o.append(val)

    y_prompt = _final_norm(xp, norm_final[None, :], min(ROW_TILE, bp * seq)).reshape(bp, seq, d)
    y_sample = _final_norm(xs, norm_final[None, :], min(ROW_TILE, bs * dseq)).reshape(bs, dseq, d)
    return (y_prompt, y_sample) + tuple(jnp.stack(o) for o in outs)
```

```python
import functools
import math

import jax
import jax.numpy as jnp
from jax import lax
from jax.experimental import pallas as pl
from jax.experimental.pallas import tpu as pltpu

F32 = jnp.float32
BF16 = jnp.bfloat16

D_MODEL = 1024
PAGE_SIZE = 128
HA, DK = 4, 64
DV = 2 * DK
A_WIDTH = HA * DV
S5_GROUP, S5_STATE, S5_WIDTH = 16, 64, 512
S5_GROUPS = S5_WIDTH // S5_GROUP
S5_LANES = S5_GROUPS * S5_STATE
HC, NC = 8, 64
C_WIDTH = HC * NC
DECAY_LORA, AAA_LORA, GATE_LORA = 64, 64, 128
C_PROJ = 3 * C_WIDTH + DECAY_LORA + AAA_LORA + GATE_LORA
N_MEM, HX, DX = 256, 4, 64
X_WIDTH = HX * DX
RMS_EPS = 1e-6
GN_EPS = 64e-5
NEG_INF = -1e30

V7X_SUBLANES = 8
V7X_VMEM_BYTES = 64 * 1024 * 1024
VMEM_LIMIT = V7X_VMEM_BYTES - 8 * 1024 * 1024

ROW_TILE = 256
ATT_TQ = 1024
ATT_TK = 512
ATT_SUB = 256
ATT_ONES_ROWS = 16
PAGES_PER_STEP = 8
SAMPLE_SEQS_PER_STEP = 2
S5_TILE = 256
S5_LANE_CHUNK = 512
S5_MAP_LANES = 128
RW_CHUNK = 64
RW_CHUNKS_PER_STEP = 4


def _params(*sem):
    return pltpu.CompilerParams(dimension_semantics=sem, vmem_limit_bytes=VMEM_LIMIT)


def _resident(shape):
    nd = len(shape)
    return pl.BlockSpec(shape, lambda *_: (0,) * nd, pipeline_mode=pl.Buffered(1))


def _rms(x, g):
    return x * lax.rsqrt(jnp.mean(x * x, axis=-1, keepdims=True) + RMS_EPS) * g


def _dot(a, b):
    return jnp.dot(a, b, preferred_element_type=F32)


def _dot_nt(a, b):
    return lax.dot_general(a, b, (((1,), (1,)), ((), ())), preferred_element_type=F32)


def _dot_tn(a, b):
    return lax.dot_general(a, b, (((0,), (0,)), ((), ())), preferred_element_type=F32)


def _split3(x):
    hi = x.astype(BF16)
    r1 = x - hi.astype(F32)
    mid = r1.astype(BF16)
    lo = (r1 - mid.astype(F32)).astype(BF16)
    return hi, mid, lo


def _dot_exact_rhs(sel, x):
    hi, mid, lo = _split3(x)
    return _dot(sel, hi) + _dot(sel, mid) + _dot(sel, lo)


def _dot_exact_lhs(x, sel):
    hi, mid, lo = _split3(x)
    return _dot(hi, sel) + _dot(mid, sel) + _dot(lo, sel)


def _norm_proj_body(splits, t_splits, x_ref, g_ref, w_ref, *rest):
    hb = _rms(x_ref[...], g_ref[...]).astype(BF16)
    out_refs = rest[1:] if t_splits else rest
    k = 0
    for off, width, outs in splits:
        y = _dot(hb, w_ref[:, off:off + width])
        for fn, _ in outs:
            out_refs[k][...] = fn(y).astype(out_refs[k].dtype)
            k += 1
    for off, width, fn, _ in t_splits:
        out_refs[k][...] = fn(_dot_nt(rest[0][off:off + width, :], hb)).astype(out_refs[k].dtype)
        k += 1


def _norm_proj(x, g, w, splits, tm, wt=None, t_splits=()):
    n, d = x.shape
    out_shape, out_specs = [], []
    for _, width, outs in splits:
        for _, dt in outs:
            out_shape.append(jax.ShapeDtypeStruct((n, width), dt))
            out_specs.append(pl.BlockSpec((tm, width), lambda i: (i, 0)))
    for _, width, _, dt in t_splits:
        out_shape.append(jax.ShapeDtypeStruct((width, n), dt))
        out_specs.append(pl.BlockSpec((width, tm), lambda i: (0, i)))
    extra = [wt] if t_splits else []
    return pl.pallas_call(
        functools.partial(_norm_proj_body, splits, tuple(t_splits)),
        grid=(n // tm,),
        in_specs=[pl.BlockSpec((tm, d), lambda i: (i, 0)), _resident(g.shape), _resident(w.shape)]
                 + [_resident(e.shape) for e in extra],
        out_specs=out_specs,
        out_shape=out_shape,
        compiler_params=_params("parallel"),
        name="norm_proj",
    )(x, g, w, *extra)


def _mem_kv_body(x_ref, g_ref, wt_ref, o_ref):
    hb = _rms(x_ref[...], g_ref[...]).astype(BF16)
    o_ref[...] = _dot_nt(wt_ref[...], hb)


def _mem_kv(mem, g, wt):
    return pl.pallas_call(
        _mem_kv_body,
        out_shape=jax.ShapeDtypeStruct((wt.shape[0], mem.shape[0]), F32),
        compiler_params=pltpu.CompilerParams(vmem_limit_bytes=VMEM_LIMIT),
        name="mem_kv",
    )(mem, g, wt)


def _ident(y):
    return y


def _scale_q(y):
    return y * (DK ** -0.5)


def _scale_q_base2(y):
    return y * (DK ** -0.5 * math.log2(math.e))


def _attn_prompt_body(post_scale, qi_ref, kj_ref, lam_ref, qt_ref, k_ref, vt_ref, g_ref, o_ref,
                      m_sc, acc_sc):
    pair = pl.program_id(1)
    i = qi_ref[pair]
    j = kj_ref[pair]
    tq, tk = qt_ref.shape[1], k_ref.shape[0]
    ratio = tq // tk

    @pl.when(j == 0)
    def _():
        m_sc[...] = jnp.full(m_sc.shape, NEG_INF, F32)
        acc_sc[...] = jnp.zeros(acc_sc.shape, F32)

    def step(diagonal):
        qt = qt_ref[...]
        k = k_ref[...]
        vt = vt_ref[...]
        feat = lax.broadcasted_iota(jnp.int32, qt.shape, 0)
        zero = jnp.zeros_like(qt)
        qtc = [jnp.where(feat < DK, qt, zero), jnp.where(feat >= DK, qt, zero)]
        key_off = (j - i * ratio) * tk
        nsub = tk // ATT_SUB

        def scores(u, c):
            s = _dot(k[u * ATT_SUB:(u + 1) * ATT_SUB], qtc[c])
            if diagonal:
                key = key_off + u * ATT_SUB + lax.broadcasted_iota(jnp.int32, s.shape, 0)
                qry = lax.broadcasted_iota(jnp.int32, s.shape, 1)
                s = jnp.where(key <= qry, s, NEG_INF)
            return s

        st = [[scores(u, c) for c in range(2)] for u in range(nsub)]
        vt1 = jnp.concatenate([vt, jnp.ones((ATT_ONES_ROWS, tk), BF16)], axis=0)
        m_run = [m_sc[c] for c in range(2)]
        for u in range(nsub):
            alpha, pb = [], []
            for c in range(2):
                m_new = jnp.maximum(m_run[c], jnp.max(st[u][c], axis=0, keepdims=True))
                alpha.append(jnp.exp2(m_run[c] - m_new))
                pb.append(jnp.exp2(st[u][c] - m_new).astype(BF16))
                m_run[c] = m_new
            for c in range(2):
                acc_sc[c] = alpha[c] * acc_sc[c] + _dot(vt1[:, u * ATT_SUB:(u + 1) * ATT_SUB], pb[c])
        for c in range(2):
            m_sc[c] = m_run[c]

    @pl.when(j < i * ratio)
    def _():
        step(False)

    @pl.when(j >= i * ratio)
    def _():
        step(True)

    @pl.when(j == (i + 1) * ratio - 1)
    def _():
        o0 = acc_sc[0, :DV, :] / acc_sc[0, DV:DV + 1, :]
        o1 = acc_sc[1, :DV, :] / acc_sc[1, DV:DV + 1, :]
        ot = o0 - lam_ref[0] * o1
        o_ref[...] = (_rms(ot.T, g_ref[...]) * post_scale).astype(o_ref.dtype)


def _attn_prompt(lam, qt, k, vt, subln, post_scale):
    n = k.shape[0]
    tk = min(ATT_TK, n)
    tq = min(ATT_TQ, n)
    ratio = tq // tk
    nq = n // tq
    qi = jnp.asarray([i for i in range(nq) for _ in range(ratio * (i + 1))], jnp.int32)
    kj = jnp.asarray([j for i in range(nq) for j in range(ratio * (i + 1))], jnp.int32)
    grid_spec = pltpu.PrefetchScalarGridSpec(
        num_scalar_prefetch=2,
        grid=(HA, qi.shape[0]),
        in_specs=[
            pl.BlockSpec(memory_space=pltpu.SMEM),
            pl.BlockSpec((DV, tq), lambda h, p, qi, kj: (h, qi[p])),
            pl.BlockSpec((tk, DV), lambda h, p, qi, kj: (kj[p], h)),
            pl.BlockSpec((DV, tk), lambda h, p, qi, kj: (h, kj[p])),
            pl.BlockSpec((1, DV), lambda h, p, qi, kj: (0, 0)),
        ],
        out_specs=pl.BlockSpec((tq, DV), lambda h, p, qi, kj: (qi[p], h)),
        scratch_shapes=[pltpu.VMEM((2, 1, tq), F32), pltpu.VMEM((2, DV + ATT_ONES_ROWS, tq), F32)],
    )
    return pl.pallas_call(
        functools.partial(_attn_prompt_body, post_scale),
        grid_spec=grid_spec,
        out_shape=jax.ShapeDtypeStruct((n, A_WIDTH), BF16),
        compiler_params=_params("parallel", "arbitrary"),
        name="attn_prompt",
    )(qi, kj, lam, qt, k, vt, subln)


def _attn_sample_body(post_scale, n_tok, pt_ref, lam_ref, q_ref, kn_ref, vn_ref, g_ref, *rest):
    del pt_ref
    npg = PAGES_PER_STEP
    nsq = SAMPLE_SEQS_PER_STEP
    k_refs = [rest[b * npg:(b + 1) * npg] for b in range(nsq)]
    v_refs = [rest[(nsq + b) * npg:(nsq + b + 1) * npg] for b in range(nsq)]
    o_ref, m_sc, l_sc, acc_sc = rest[2 * nsq * npg:]
    j = pl.program_id(1)
    hrows = 2 * n_tok
    nrow = HA * hrows
    width = HA * 2 * DK

    rid = lax.broadcasted_iota(jnp.int32, (nrow, width), 0)
    cid = lax.broadcasted_iota(jnp.int32, (nrow, width), 1)
    qblk = []
    for b in range(nsq):
        qrep = jnp.concatenate([q_ref[b]] * (2 * HA), axis=0)
        qblk.append(jnp.where(cid // DK == rid // n_tok, qrep, jnp.zeros_like(qrep)).astype(BF16))

    @pl.when(j == 0)
    def _():
        m_sc[...] = jnp.full(m_sc.shape, NEG_INF, F32)
        l_sc[...] = jnp.zeros(l_sc.shape, F32)
        acc_sc[...] = jnp.zeros(acc_sc.shape, F32)

    def update(s_pages, v_pages):
        seqs = range(nsq)
        m_prev = [m_sc[b] for b in seqs]
        s_max = []
        for b in seqs:
            s_red = s_pages[b][0]
            for s in s_pages[b][1:]:
                s_red = jnp.maximum(s_red, s)
            s_max.append(s_red)
        m_new = [jnp.maximum(m_prev[b], jnp.max(s_max[b], axis=-1, keepdims=True)) for b in seqs]
        alpha = [jnp.exp(m_prev[b] - m_new[b]) for b in seqs]
        p_sum = [None] * nsq
        pv = [[None] * HA for _ in seqs]
        for i in range(len(s_pages[0])):
            for b in seqs:
                p = jnp.exp(s_pages[b][i] - m_new[b])
                p_sum[b] = p if p_sum[b] is None else p_sum[b] + p
                pb = p.astype(BF16)
                for h in range(HA):
                    d = _dot(pb[h * hrows:(h + 1) * hrows], v_pages[b][i][h])
                    pv[b][h] = d if pv[b][h] is None else pv[b][h] + d
        for b in seqs:
            l_sc[b] = alpha[b] * l_sc[b] + jnp.sum(p_sum[b], axis=-1, keepdims=True)
            acc_sc[b] = alpha[b] * acc_sc[b] + jnp.concatenate(pv[b], axis=0)
            m_sc[b] = m_new[b]

    update([[_dot(qblk[b], k_refs[b][i][...].astype(BF16)) for i in range(npg)] for b in range(nsq)],
           [[[v_refs[b][i][pl.ds(h, PAGE_SIZE, stride=HA), :].astype(BF16) for h in range(HA)]
             for i in range(npg)] for b in range(nsq)])

    @pl.when(j == pl.num_programs(1) - 1)
    def _():
        pad = jnp.zeros((PAGE_SIZE - n_tok, width), F32)
        r2 = lax.broadcasted_iota(jnp.int32, (nrow, PAGE_SIZE), 0) % n_tok
        c2 = lax.broadcasted_iota(jnp.int32, (nrow, PAGE_SIZE), 1)
        s_new, v_new = [], []
        for b in range(nsq):
            kn = jnp.concatenate([kn_ref[b], pad], axis=0).astype(BF16)
            vn = jnp.concatenate([vn_ref[b], pad], axis=0).astype(BF16)
            s_new.append([jnp.where(c2 <= r2, _dot_nt(qblk[b], kn), NEG_INF)])
            v_new.append([[vn[:, h * DV:(h + 1) * DV] for h in range(HA)]])
        update(s_new, v_new)
        lam = lam_ref[0]
        for b in range(nsq):
            sel = acc_sc[b] / l_sc[b]
            heads = []
            for h in range(HA):
                o = sel[h * hrows:h * hrows + n_tok] - lam * sel[h * hrows + n_tok:(h + 1) * hrows]
                heads.append(_rms(o, g_ref[...]) * post_scale)
            o_ref[b] = jnp.concatenate(heads, axis=1).astype(o_ref.dtype)


def _attn_sample(layer, page_table, lam, q, k_new, v_new, subln, cache_kt, cache_vr, post_scale):
    b, n_tok, width = q.shape
    n_pages = page_table.shape[1]
    npg = PAGES_PER_STEP
    nsq = SAMPLE_SEQS_PER_STEP
    nj = n_pages // npg
    pt = page_table.reshape(-1)

    def page_spec(s, i):
        return pl.BlockSpec((None, None, width, PAGE_SIZE),
                            lambda bb, j, p: (layer, p[(bb * nsq + s) * n_pages + j * npg + i], 0, 0))

    pages = [page_spec(s, i) for s in range(nsq) for i in range(npg)]
    tok_spec = pl.BlockSpec((nsq, n_tok, width), lambda bb, j, p: (bb, 0, 0))
    nrow = 2 * HA * n_tok
    grid_spec = pltpu.PrefetchScalarGridSpec(
        num_scalar_prefetch=1,
        grid=(b // nsq, nj),
        in_specs=[pl.BlockSpec(memory_space=pltpu.SMEM), tok_spec, tok_spec, tok_spec,
                  pl.BlockSpec((1, DV), lambda bb, j, p: (0, 0))] + pages + pages,
        out_specs=tok_spec,
        scratch_shapes=[pltpu.VMEM((nsq, nrow, 1), F32), pltpu.VMEM((nsq, nrow, 1), F32),
                        pltpu.VMEM((nsq, nrow, DV), F32)],
    )
    n_ops = nsq * npg
    return pl.pallas_call(
        functools.partial(_attn_sample_body, post_scale, n_tok),
        grid_spec=grid_spec,
        out_shape=jax.ShapeDtypeStruct((b, n_tok, width), F32),
        compiler_params=_params("parallel", "arbitrary"),
        name="attn_sample",
    )(pt, lam, q, k_new, v_new, subln, *([cache_kt] * n_ops), *([cache_vr] * n_ops))


def _s5_body(chained, u_ref, x0r_ref, x0i_ref, tab_ref, bre_ref, bim_ref, cre_ref, cim_ref,
             d_ref, wg_ref, bg_ref, y_ref, xfr_ref, xfi_ref, sr_sc, si_sc, st_sc):
    i = pl.program_id(0)
    tt = u_ref.shape[0]
    sub = V7X_SUBLANES
    u = u_ref[...]
    ub = u.astype(BF16)
    nblk = bre_ref.shape[0]
    cw, sw = S5_WIDTH // nblk, S5_LANES // nblk
    for q in range(nblk):
        sr_sc[sub:, q * sw:(q + 1) * sw] = _dot(ub[:, q * cw:(q + 1) * cw], bre_ref[q])
        si_sc[sub:, q * sw:(q + 1) * sw] = _dot(ub[:, q * cw:(q + 1) * cw], bim_ref[q])

    if chained:
        @pl.when(i == 0)
        def _():
            st_sc[0:1, :] = x0r_ref[...]
            st_sc[1:2, :] = x0i_ref[...]

        sr_sc[sub - 1:sub, :] = st_sc[0:1, :]
        si_sc[sub - 1:sub, :] = st_sc[1:2, :]

    def group(gi, carry):
        r0 = pl.multiple_of(gi * sub, sub)
        for lc in range(S5_LANES // S5_LANE_CHUNK):
            ls = slice(lc * S5_LANE_CHUNK, (lc + 1) * S5_LANE_CHUNK)
            vr = sr_sc[pl.ds(r0 + sub, sub), ls]
            vi = si_sc[pl.ds(r0 + sub, sub), ls]
            for t, d in enumerate((1, 2, 4)):
                ar, ai = tab_ref[2 * t, :, ls], tab_ref[2 * t + 1, :, ls]
                pr, pi = pltpu.roll(vr, d, axis=0), pltpu.roll(vi, d, axis=0)
                vr, vi = vr + ar * pr - ai * pi, vi + ar * pi + ai * pr
            if chained:
                cr = sr_sc[pl.ds(r0 + sub - 1, 1), ls]
                ci = si_sc[pl.ds(r0 + sub - 1, 1), ls]
            else:
                cr = x0r_ref[pl.ds(gi, 1), ls]
                ci = x0i_ref[pl.ds(gi, 1), ls]
            cr = jnp.broadcast_to(cr, vr.shape)
            ci = jnp.broadcast_to(ci, vi.shape)
            pr, pi = tab_ref[6, :, ls], tab_ref[7, :, ls]
            vr, vi = vr + pr * cr - pi * ci, vi + pr * ci + pi * cr
            sr_sc[pl.ds(r0 + sub, sub), ls] = vr
            si_sc[pl.ds(r0 + sub, sub), ls] = vi
            if not chained:
                xfr_ref[pl.ds(gi, 1), ls] = vr[sub - 1:sub]
                xfi_ref[pl.ds(gi, 1), ls] = vi[sub - 1:sub]
        return carry

    lax.fori_loop(0, tt // sub, group, 0)

    if chained:
        st_sc[0:1, :] = sr_sc[tt + sub - 1:tt + sub, :]
        st_sc[1:2, :] = si_sc[tt + sub - 1:tt + sub, :]
        xfr_ref[...] = st_sc[0:1, :]
        xfi_ref[...] = st_sc[1:2, :]

    cx = []
    for q in range(nblk):
        xr = sr_sc[sub:, q * sw:(q + 1) * sw].astype(BF16)
        xi = si_sc[sub:, q * sw:(q + 1) * sw].astype(BF16)
        cx.append(_dot(xr, cre_ref[q]) - _dot(xi, cim_ref[q]))
    y = jnp.concatenate(cx, axis=1) + d_ref[...] * u
    z = y * (0.5 * (1.0 + jnp.tanh(math.sqrt(2.0 / math.pi) * (y + 0.044715 * (y * y * y)))))
    gate = jax.nn.sigmoid(_dot(z.astype(BF16), wg_ref[...]) + bg_ref[...])
    y_ref[...] = (z * gate).astype(y_ref.dtype)


def _s5(u, x0r, x0i, sp, chained):
    n = u.shape[0]
    tt = min(S5_TILE, n)
    ng = x0r.shape[0]
    if chained:
        st_spec = pl.BlockSpec((1, S5_LANES), lambda i: (0, 0))
    else:
        assert tt == n and ng * V7X_SUBLANES == n
        st_spec = pl.BlockSpec((ng, S5_LANES), lambda i: (0, 0))
    consts = [sp["tab"], sp["b_re"], sp["b_im"], sp["c_re"], sp["c_im"], sp["d"], sp["w_glu"], sp["b_glu"]]
    return pl.pallas_call(
        functools.partial(_s5_body, chained),
        grid=(n // tt,),
        in_specs=[pl.BlockSpec((tt, S5_WIDTH), lambda i: (i, 0)), st_spec, st_spec]
                 + [_resident(c.shape) for c in consts],
        out_specs=[pl.BlockSpec((tt, S5_WIDTH), lambda i: (i, 0)), st_spec, st_spec],
        out_shape=[jax.ShapeDtypeStruct((n, S5_WIDTH), BF16),
                   jax.ShapeDtypeStruct((ng, S5_LANES), F32),
                   jax.ShapeDtypeStruct((ng, S5_LANES), F32)],
        scratch_shapes=[pltpu.VMEM((tt + V7X_SUBLANES, S5_LANES), F32),
                        pltpu.VMEM((tt + V7X_SUBLANES, S5_LANES), F32),
                        pltpu.VMEM((2, S5_LANES), F32)],
        compiler_params=_params("arbitrary"),
        name="s5_branch",
    )(u, x0r, x0i, *consts)


def _rwkv_body(t_valid, pc_ref, prev_ref, sh_ref, s0_ref, mu_ref, w0_ref, w2_ref, a0_ref, a2_ref,
               g2_ref, kk_ref, ka_ref, rk_ref, lnw_ref, lnb_ref, ones_ref,
               y_ref, sf_ref, s_sc):
    j = pl.program_id(1)
    t = RW_CHUNK
    rows_in = pc_ref.shape[0]
    tp = max(rows_in, t)
    nch = tp // t

    @pl.when(j == 0)
    def _():
        s_sc[...] = s0_ref[...]

    pc = pc_ref[...]
    if rows_in < tp:
        pc = jnp.concatenate([pc, jnp.zeros((tp - rows_in, C_PROJ), F32)], axis=0)
    prev_row = jnp.where(j == 0, sh_ref[...], prev_ref[V7X_SUBLANES - 1:V7X_SUBLANES, :])
    row_id = lax.broadcasted_iota(jnp.int32, (tp, C_PROJ), 0)
    shifted = jnp.where(row_id == 0, jnp.broadcast_to(prev_row, pc.shape), pltpu.roll(pc, 1, axis=0))
    xm = pc + (shifted - pc) * mu_ref[...]

    c = C_WIDTH
    r = xm[:, 0:c]
    k = xm[:, c:2 * c]
    v = xm[:, 2 * c:3 * c]
    pw = xm[:, 3 * c:3 * c + DECAY_LORA]
    pa = xm[:, 3 * c + DECAY_LORA:3 * c + DECAY_LORA + AAA_LORA]
    pg = xm[:, 3 * c + DECAY_LORA + AAA_LORA:]

    wl = w0_ref[...] + _dot(jnp.tanh(pw).astype(BF16), w2_ref[...])
    neg = -wl
    softplus = jnp.maximum(neg, 0.0) + jnp.log1p(jnp.exp(-jnp.abs(neg)))
    logw = -jnp.exp(-softplus - 0.5)
    a = jax.nn.sigmoid(a0_ref[...] + _dot(pa.astype(BF16), a2_ref[...]))
    g = _dot(jax.nn.sigmoid(pg).astype(BF16), g2_ref[...])

    ones = ones_ref[...]
    kk = k * kk_ref[...]
    kk = kk / jnp.maximum(jnp.sqrt(_dot_exact_lhs(kk * kk, ones)), 1e-12)
    kh = k * (1.0 + (a - 1.0) * ka_ref[...])
    alpha = -kk
    beta = kk * a

    if t_valid < tp:
        valid = lax.broadcasted_iota(jnp.int32, (tp, c), 0) < t_valid
        zero = jnp.zeros((tp, c), F32)
        logw = jnp.where(valid, logw, zero)
        alpha = jnp.where(valid, alpha, zero)
        beta = jnp.where(valid, beta, zero)
        kh_m = jnp.where(valid, kh, zero)
        v_m = jnp.where(valid, v, zero)
    else:
        kh_m, v_m = kh, v

    ri = lax.broadcasted_iota(jnp.int32, (tp, tp), 0)
    ci = lax.broadcasted_iota(jnp.int32, (tp, tp), 1)
    blocktril = jnp.logical_and(ri // t == ci // t, ci <= ri).astype(BF16)
    cum = _dot_exact_rhs(blocktril, logw)
    c_end = jnp.concatenate(
        [jnp.broadcast_to(cum[(q + 1) * t - 1:(q + 1) * t, :], (t, c)) for q in range(nch)], axis=0)
    e_cum = jnp.exp(cum)
    e_neg = jnp.exp(-cum)
    e_tail = jnp.exp(c_end - cum)
    at = (alpha * jnp.exp(cum - logw)).astype(BF16)
    rt = r * e_cum
    rtb = rt.astype(BF16)
    bt = (beta * e_neg).astype(BF16)
    kt = (kh_m * e_neg).astype(BF16)
    bh = (beta * e_tail).astype(BF16)
    kx = (kh_m * e_tail).astype(BF16)
    vb = v_m.astype(BF16)
    w_end = jnp.exp(c_end)

    ti = lax.broadcasted_iota(jnp.int32, (t, t), 0)
    si = lax.broadcasted_iota(jnp.int32, (t, t), 1)
    incl = si <= ti
    strict = si < ti
    eye = (si == ti).astype(F32)

    pairs = [(q, h) for q in range(nch) for h in range(HC)]

    def blk(x, q, h):
        return x[q * t:(q + 1) * t, h * NC:(h + 1) * NC]

    def stage(fn):
        return [fn(q, h, i) for i, (q, h) in enumerate(pairs)]

    ar = stage(lambda q, h, i: jnp.concatenate([blk(at, q, h), blk(rtb, q, h)], axis=0))
    xb = stage(lambda q, h, i: _dot_nt(ar[i], blk(bt, q, h)))
    xk = stage(lambda q, h, i: _dot_nt(ar[i], blk(kt, q, h)))
    n_ab = stage(lambda q, h, i: jnp.where(strict, xb[i][:t], 0.0))
    a_br = stage(lambda q, h, i: jnp.where(incl, xb[i][t:], 0.0).astype(BF16))
    a_ak = stage(lambda q, h, i: jnp.where(strict, xk[i][:t], 0.0).astype(BF16))
    a_kr = stage(lambda q, h, i: jnp.where(incl, xk[i][t:], 0.0).astype(BF16))
    minv = stage(lambda q, h, i: eye + n_ab[i])
    pw2 = n_ab
    span = 1
    while span * 2 < t:
        pb = [x.astype(BF16) for x in pw2]
        pw2 = stage(lambda q, h, i: _dot(pb[i], pb[i]))
        minv = stage(lambda q, h, i: minv[i] + _dot(pw2[i].astype(BF16), minv[i].astype(BF16)))
        span *= 2
    minvb = [x.astype(BF16) for x in minv]
    akv = stage(lambda q, h, i: _dot(a_ak[i], blk(vb, q, h)).astype(BF16))
    p1 = stage(lambda q, h, i: _dot(minvb[i], blk(at, q, h)).astype(BF16))
    p2 = stage(lambda q, h, i: _dot(minvb[i], akv[i]).astype(BF16))
    qm = stage(lambda q, h, i: (blk(rt, q, h) + _dot(a_br[i], p1[i])).astype(BF16))
    zm = stage(lambda q, h, i: _dot(a_br[i], p2[i]) + _dot(a_kr[i], blk(vb, q, h)))
    gm = stage(lambda q, h, i: _dot_tn(p1[i], blk(bh, q, h)).astype(BF16))
    hm = stage(lambda q, h, i: _dot_tn(p2[i], blk(bh, q, h)) + _dot_tn(blk(vb, q, h), blk(kx, q, h)))

    s_cur = [s_sc[h] for h in range(HC)]
    y_rows = []
    for q in range(nch):
        ys = []
        for h in range(HC):
            i = q * HC + h
            sb = s_cur[h].astype(BF16)
            ys.append(_dot_nt(qm[i], sb) + zm[i])
            s_cur[h] = s_cur[h] * blk(w_end, q, h)[0:1, :] + _dot(sb, gm[i]) + hm[i]
        y_rows.append(jnp.concatenate(ys, axis=1))
    for h in range(HC):
        s_sc[h] = s_cur[h]
        sf_ref[h] = s_cur[h]

    y = y_rows[0] if nch == 1 else jnp.concatenate(y_rows, axis=0)
    inv_n = 1.0 / NC
    mean = _dot_exact_lhs(y, ones) * inv_n
    dlt = y - mean
    var = _dot_exact_lhs(dlt * dlt, ones) * inv_n
    yn = dlt * lax.rsqrt(var + GN_EPS) * lnw_ref[...] + lnb_ref[...]
    bonus = _dot_exact_lhs(r * kh * rk_ref[...], ones) * v
    out = (yn + bonus) * g
    y_ref[...] = out[:rows_in].astype(y_ref.dtype)


def _rwkv(pc, shift_prev, s0, rp, batch, seq):
    rows = min(RW_CHUNK * RW_CHUNKS_PER_STEP, seq)
    nt = seq // rows
    sub = V7X_SUBLANES
    per8 = seq // sub
    consts = [rp["mu"], rp["w0"], rp["w2"], rp["a0"], rp["a2"], rp["g2"], rp["k_k"], rp["k_a"],
              rp["r_k"], rp["ln_w"], rp["ln_b"], rp["ones"]]
    return pl.pallas_call(
        functools.partial(_rwkv_body, rows),
        grid=(batch, nt),
        in_specs=[
            pl.BlockSpec((rows, C_PROJ), lambda b, j: (b * nt + j, 0)),
            pl.BlockSpec((sub, C_PROJ), lambda b, j: (b * per8 + jnp.maximum(j * (rows // sub) - 1, 0), 0)),
            pl.BlockSpec((None, 1, C_PROJ), lambda b, j: (b, 0, 0)),
            pl.BlockSpec((None, HC, NC, NC), lambda b, j: (b, 0, 0, 0)),
        ] + [_resident(c.shape) for c in consts],
        out_specs=[pl.BlockSpec((rows, C_WIDTH), lambda b, j: (b * nt + j, 0)),
                   pl.BlockSpec((None, HC, NC, NC), lambda b, j: (b, 0, 0, 0))],
        out_shape=[jax.ShapeDtypeStruct((batch * seq, C_WIDTH), F32),
                   jax.ShapeDtypeStruct((batch, HC, NC, NC), F32)],
        scratch_shapes=[pltpu.VMEM((HC, NC, NC), F32)],
        compiler_params=_params("parallel", "arbitrary"),
        name="rwkv_branch",
    )(pc, pc, shift_prev, s0, *consts)


def _merge_body(x_ref, sg_ref, ya_ref, yb_ref, yc_ref, pa_ref, pb_ref, pc_ref, wo_ref, o_ref):
    d = D_MODEL
    sg = sg_ref[...]
    merged = (sg[:, 0:d] * _dot(ya_ref[...].astype(BF16), pa_ref[...])
              + sg[:, d:2 * d] * _dot(yb_ref[...].astype(BF16), pb_ref[...])
              + sg[:, 2 * d:3 * d] * _dot(yc_ref[...].astype(BF16), pc_ref[...]))
    o_ref[...] = x_ref[...] + _dot(merged.astype(BF16), wo_ref[...])


def _merge(x, sg, ya, yb, yc, wp, tm):
    n = x.shape[0]
    rows = lambda w: pl.BlockSpec((tm, w), lambda i: (i, 0))
    consts = [wp["proj_a"], wp["proj_b"], wp["proj_c"], wp["w_out"]]
    return pl.pallas_call(
        _merge_body,
        grid=(n // tm,),
        in_specs=[rows(D_MODEL), rows(3 * D_MODEL), rows(A_WIDTH), rows(S5_WIDTH), rows(C_WIDTH)]
                 + [_resident(c.shape) for c in consts],
        out_specs=rows(D_MODEL),
        out_shape=jax.ShapeDtypeStruct((n, D_MODEL), F32),
        compiler_params=_params("parallel"),
        name="merge_out",
    )(x, sg, ya, yb, yc, *consts)


def _cross_body(x_ref, g_ref, wq_ref, mk_ref, mv_ref, wo_ref, o_ref):
    x = x_ref[...]
    q = _dot(_rms(x, g_ref[...]).astype(BF16), wq_ref[...])
    mk = mk_ref[...].astype(BF16)
    mv = mv_ref[...].astype(BF16)
    heads = []
    for h in range(HX):
        hs = slice(h * DX, (h + 1) * DX)
        s = _dot(q[:, hs].astype(BF16), mk[hs, :]) * (DX ** -0.5)
        e = jnp.exp(s - jnp.max(s, axis=-1, keepdims=True))
        p = e / jnp.sum(e, axis=-1, keepdims=True)
        heads.append(_dot_nt(p.astype(BF16), mv[hs, :]))
    o = jnp.concatenate(heads, axis=1)
    o_ref[...] = x + _dot(o.astype(BF16), wo_ref[...])


def _cross(x, mem_k, mem_v, wp, batch, seq, tm):
    nt = seq // tm
    xs = pl.BlockSpec((tm, D_MODEL), lambda b, j: (b * nt + j, 0))
    ms = pl.BlockSpec((None, X_WIDTH, N_MEM), lambda b, j: (b, 0, 0))
    consts_a = [wp["norm_x"], wp["w_xq"]]
    return pl.pallas_call(
        _cross_body,
        grid=(batch, nt),
        in_specs=[xs] + [_resident(c.shape) for c in consts_a] + [ms, ms, _resident(wp["w_xo"].shape)],
        out_specs=xs,
        out_shape=jax.ShapeDtypeStruct((batch * seq, D_MODEL), F32),
        compiler_params=_params("parallel", "parallel"),
        name="cross_attn",
    )(x, *consts_a, mem_k, mem_v, wp["w_xo"])


def _ffn_body(x_ref, g_ref, wg_ref, wu_ref, wd_ref, o_ref):
    x = x_ref[...]
    hb = _rms(x, g_ref[...]).astype(BF16)
    gate = _dot(hb, wg_ref[...])
    up = _dot(hb, wu_ref[...])
    act = (gate * jax.nn.sigmoid(gate) * up).astype(BF16)
    o_ref[...] = x + _dot(act, wd_ref[...])


def _ffn(x, wp, tm):
    n = x.shape[0]
    rows = pl.BlockSpec((tm, D_MODEL), lambda i: (i, 0))
    consts = [wp["norm_ffn"], wp["w_gate"], wp["w_up"], wp["w_down"]]
    return pl.pallas_call(
        _ffn_body,
        grid=(n // tm,),
        in_specs=[rows] + [_resident(c.shape) for c in consts],
        out_specs=rows,
        out_shape=jax.ShapeDtypeStruct((n, D_MODEL), F32),
        compiler_params=_params("parallel"),
        name="swiglu",
    )(x, *consts)


def _final_norm_body(x_ref, g_ref, o_ref):
    o_ref[...] = _rms(x_ref[...], g_ref[...])


def _final_norm(x, g, tm):
    n = x.shape[0]
    rows = pl.BlockSpec((tm, D_MODEL), lambda i: (i, 0))
    return pl.pallas_call(
        _final_norm_body,
        grid=(n // tm,),
        in_specs=[rows, _resident(g.shape)],
        out_specs=rows,
        out_shape=jax.ShapeDtypeStruct((n, D_MODEL), F32),
        compiler_params=_params("parallel"),
        name="final_norm",
    )(x, g)


def _s5_params(lp):
    lam_re, lam_im = lp["s5_lam_re"], lp["s5_lam_im"]
    dt = jnp.exp(lp["s5_log_dt"])[:, None]
    mag = jnp.exp(lam_re * dt)
    ab_re = mag * jnp.cos(lam_im * dt)
    ab_im = mag * jnp.sin(lam_im * dt)
    den = lam_re * lam_re + lam_im * lam_im
    nr = ab_re - 1.0
    cr = (nr * lam_re + ab_im * lam_im) / den
    ci = (ab_im * lam_re - nr * lam_im) / den
    bb_re = cr[..., None] * lp["s5_b_re"] - ci[..., None] * lp["s5_b_im"]
    bb_im = cr[..., None] * lp["s5_b_im"] + ci[..., None] * lp["s5_b_re"]
    gpb = S5_MAP_LANES // S5_GROUP
    nblk = S5_GROUPS // gpb
    eye = jnp.eye(gpb, dtype=F32)

    def blockdiag_in(b):
        b = b.reshape(nblk, gpb, S5_STATE, S5_GROUP)
        return jnp.einsum("qgph,gk->qghkp", b, eye).reshape(nblk, gpb * S5_GROUP, gpb * S5_STATE)

    def blockdiag_out(cm):
        cm = cm.reshape(nblk, gpb, S5_GROUP, S5_STATE)
        return jnp.einsum("qghp,gk->qgpkh", cm, eye).reshape(nblk, gpb * S5_STATE, gpb * S5_GROUP)

    ar, ai = ab_re.reshape(-1), ab_im.reshape(-1)
    pows_r, pows_i = [ar], [ai]
    for _ in range(V7X_SUBLANES - 1):
        pr, pi = pows_r[-1], pows_i[-1]
        pows_r.append(pr * ar - pi * ai)
        pows_i.append(pr * ai + pi * ar)
    rows = jnp.arange(V7X_SUBLANES)[:, None]
    tabs = []
    for d in (1, 2, 4):
        tabs.append(jnp.where(rows >= d, pows_r[d - 1][None, :], 0.0))
        tabs.append(jnp.where(rows >= d, pows_i[d - 1][None, :], 0.0))
    tabs.append(jnp.stack(pows_r))
    tabs.append(jnp.stack(pows_i))
    return {
        "tab": jnp.stack(tabs).astype(F32),
        "b_re": blockdiag_in(bb_re).astype(BF16), "b_im": blockdiag_in(bb_im).astype(BF16),
        "c_re": blockdiag_out(lp["s5_c_re"]).astype(BF16), "c_im": blockdiag_out(lp["s5_c_im"]).astype(BF16),
        "d": lp["s5_d"][None, :], "w_glu": lp["s5_w_glu"].astype(BF16), "b_glu": lp["s5_b_glu"][None, :],
    }


def _rwkv_params(lp):
    seg = jnp.arange(C_WIDTH) // NC
    row = lambda a: a.reshape(1, -1)
    return {
        "mu": row(lp["rw_mu"]), "w0": row(lp["rw_w0"]), "w2": lp["rw_w2"].astype(BF16),
        "a0": row(lp["rw_a0"]), "a2": lp["rw_a2"].astype(BF16), "g2": lp["rw_g2"].astype(BF16),
        "k_k": row(lp["rw_k_k"]), "k_a": row(lp["rw_k_a"]), "r_k": row(lp["rw_r_k"]),
        "ln_w": row(lp["rw_ln_w"]), "ln_b": row(lp["rw_ln_b"]),
        "ones": (seg[:, None] == seg[None, :]).astype(BF16),
    }


def _in_proj(x, lp, tm, feature_major_qv):
    qa = HA * 2 * DK
    o_u = 3 * qa
    o_pc = o_u + S5_WIDTH
    o_g = o_pc + C_PROJ
    tail = [
        (o_u, S5_WIDTH, [(_ident, F32)]),
        (o_pc, C_PROJ, [(_ident, F32)]),
        (o_g, 3 * D_MODEL, [(jax.nn.sigmoid, F32)]),
    ]
    if feature_major_qv:
        splits = [(qa, qa, [(_ident, F32), (_ident, BF16)]), (2 * qa, qa, [(_ident, F32)])] + tail
        t_splits = [(0, qa, _scale_q_base2, BF16), (qa, qa, _ident, BF16)]
        k_f, k_b, v_f, u, pc, sg, q_t, v_t = _norm_proj(x, lp["norm_mix"], lp["w_in_b"], splits, tm,
                                                       wt=lp["w_qv_t"], t_splits=t_splits)
        return q_t, k_f, k_b, v_f, v_t, u, pc, sg
    splits = [(0, qa, [(_scale_q, F32)]), (qa, qa, [(_ident, F32)]), (2 * qa, qa, [(_ident, F32)])] + tail
    q, k_f, v_f, u, pc, sg = _norm_proj(x, lp["norm_mix"], lp["w_in_b"], splits, tm)
    return q, k_f, None, v_f, None, u, pc, sg


def _layer(l, x, batch, seq, attn_fn, mem_k, mem_v, ssm_re0, ssm_im0, rwkv0, shift0, lp):
    n = batch * seq
    tm = min(ROW_TILE, n)
    lam_init = 0.8 - 0.6 * math.exp(-0.3 * l)
    lam = (jnp.exp(jnp.sum(lp["lam_q1"] * lp["lam_k1"])) - jnp.exp(jnp.sum(lp["lam_q2"] * lp["lam_k2"]))
           + lam_init).reshape(1).astype(F32)
    q, k_f, k_b, v_f, v_t, u, pc, sg = _in_proj(x, lp, tm, feature_major_qv=(batch == 1))
    ya = attn_fn(lam, q, k_f, k_b, v_f, v_t, lp["subln"], 1.0 - lam_init)
    yb, ssm_re, ssm_im = _s5(u, ssm_re0, ssm_im0, lp["s5"], chained=(batch == 1))
    yc, rwkv_s = _rwkv(pc, shift0, rwkv0, lp["rw"], batch, seq)
    shift = pc.reshape(batch, seq, C_PROJ)[:, -1]
    x = _merge(x, sg, ya, yb, yc, lp, tm)
    x = _cross(x, mem_k, mem_v, lp, batch, seq, min(tm, seq))
    x = _ffn(x, lp, tm)
    return x, k_f, v_f, ssm_re, ssm_im, rwkv_s, shift


def kernel(x_prompt, x_sample, cache_k, cache_v, cache_mem_k, cache_mem_v, state_ssm_re, state_ssm_im, state_rwkv, state_shift, page_table, mem_prompt, norm_mix, w_in, lam_q1, lam_k1, lam_q2, lam_k2, subln, proj_a, s5_lam_re, s5_lam_im, s5_log_dt, s5_b_re, s5_b_im, s5_c_re, s5_c_im, s5_d, s5_w_glu, s5_b_glu, proj_b, rw_mu, rw_w0, rw_w2, rw_a0, rw_a2, rw_g2, rw_k_k, rw_k_a, rw_r_k, rw_ln_w, rw_ln_b, proj_c, w_out, norm_x, norm_mem, w_xq, w_mk, w_mv, w_xo, norm_ffn, w_gate, w_up, w_down, norm_final):
    depth = w_in.shape[0]
    bp, seq, d = x_prompt.shape
    bs, dseq, _ = x_sample.shape
    assert bp == 1
    pool = cache_k.shape[1]
    ck = cache_k.transpose(0, 1, 3, 4, 5, 2).reshape(depth, pool, HA * 2 * DK, PAGE_SIZE)
    cv = cache_v.reshape(depth, pool, PAGE_SIZE * HA, DV)

    xp = x_prompt.reshape(bp * seq, d)
    xs = x_sample.reshape(bs * dseq, d)
    zeros_ssm = jnp.zeros((bp, S5_LANES), F32)
    zeros_rwkv = jnp.zeros((bp, HC, NC, NC), F32)
    zeros_shift = jnp.zeros((bp, 1, C_PROJ), F32)
    outs = [[] for _ in range(14)]
    for l in range(depth):
        raw = dict(lam_q1=lam_q1[l], lam_k1=lam_k1[l], lam_q2=lam_q2[l], lam_k2=lam_k2[l],
                   s5_lam_re=s5_lam_re[l], s5_lam_im=s5_lam_im[l], s5_log_dt=s5_log_dt[l],
                   s5_b_re=s5_b_re[l], s5_b_im=s5_b_im[l], s5_c_re=s5_c_re[l], s5_c_im=s5_c_im[l],
                   s5_d=s5_d[l], s5_w_glu=s5_w_glu[l], s5_b_glu=s5_b_glu[l],
                   rw_mu=rw_mu[l], rw_w0=rw_w0[l], rw_w2=rw_w2[l], rw_a0=rw_a0[l], rw_a2=rw_a2[l],
                   rw_g2=rw_g2[l], rw_k_k=rw_k_k[l], rw_k_a=rw_k_a[l], rw_r_k=rw_r_k[l].reshape(-1),
                   rw_ln_w=rw_ln_w[l], rw_ln_b=rw_ln_b[l])
        lp = dict(raw)
        lp.update(
            norm_mix=norm_mix[l][None, :], w_in_b=w_in[l].astype(BF16), subln=subln[l][None, :],
            w_qv_t=jnp.concatenate([w_in[l][:, :HA * 2 * DK], w_in[l][:, 2 * HA * 2 * DK:3 * HA * 2 * DK]],
                                   axis=1).T.astype(BF16),
            proj_a=proj_a[l].astype(BF16), proj_b=proj_b[l].astype(BF16), proj_c=proj_c[l].astype(BF16),
            w_out=w_out[l].astype(BF16), norm_x=norm_x[l][None, :], w_xq=w_xq[l].astype(BF16),
            w_xo=w_xo[l].astype(BF16), norm_ffn=norm_ffn[l][None, :], w_gate=w_gate[l].astype(BF16),
            w_up=w_up[l].astype(BF16), w_down=w_down[l].astype(BF16),
            s5=_s5_params(raw), rw=_rwkv_params(raw))

        w_mkv_t = jnp.concatenate([w_mk[l], w_mv[l]], axis=1).T.astype(BF16)
        mkv_t = _mem_kv(mem_prompt.reshape(bp * N_MEM, d), norm_mem[l][None, :], w_mkv_t)
        mk_t = mkv_t[:X_WIDTH].reshape(bp, X_WIDTH, N_MEM)
        mv_t = mkv_t[X_WIDTH:].reshape(bp, X_WIDTH, N_MEM)

        def attn_p(lam, q_t, k_f, k_b, v_f, v_t, sub_g, post):
            return _attn_prompt(lam, q_t, k_b, v_t, sub_g, post)

        xp, kp, vp, srp, sip, rwp, shp = _layer(
            l, xp, bp, seq, attn_p, mk_t, mv_t, zeros_ssm, zeros_ssm, zeros_rwkv, zeros_shift, lp)

        def feature_major(mem):
            return mem.transpose(0, 2, 3, 1).reshape(mem.shape[0], X_WIDTH, N_MEM)

        def token_major(mem_t):
            return mem_t.reshape(bp, HX, DX, N_MEM).transpose(0, 3, 1, 2)

        def attn_s(lam, q, k_f, k_b, v_f, v_t, sub_g, post, _l=l):
            shp3 = (bs, dseq, HA * 2 * DK)
            return _attn_sample(_l, page_table, lam, q.reshape(shp3), k_f.reshape(shp3),
                                v_f.reshape(shp3), sub_g, ck, cv, post).reshape(bs * dseq, A_WIDTH)

        xs, ks, vs, srs, sis, rws, shs = _layer(
            l, xs, bs, dseq, attn_s, feature_major(cache_mem_k[l]), feature_major(cache_mem_v[l]),
            state_ssm_re[l].reshape(bs, S5_LANES), state_ssm_im[l].reshape(bs, S5_LANES),
            state_rwkv[l], state_shift[l].reshape(bs, 1, C_PROJ), lp)

        vals = [kp.reshape(bp, seq, HA, 2, DK), vp.reshape(bp, seq, HA, DV),
                token_major(mk_t), token_major(mv_t),
                srp.reshape(bp, S5_GROUPS, S5_STATE), sip.reshape(bp, S5_GROUPS, S5_STATE), rwp, shp,
                ks.reshape(bs, dseq, HA, 2, DK), vs.reshape(bs, dseq, HA, DV),
                srs.reshape(bs, S5_GROUPS, S5_STATE), sis.reshape(bs, S5_GROUPS, S5_STATE), rws, shs]
        for o, val in zip(outs, vals):
            o.append(val)

    y_prompt = _final_norm(xp, norm_final[None, :], min(ROW_TILE, bp * seq)).reshape(bp, seq, d)
    y_sample = _final_norm(xs, norm_final[None, :], min(ROW_TILE, bs * dseq)).reshape(bs, dseq, d)
    return (y_prompt, y_sample) + tuple(jnp.stack(o) for o in outs)
```

```python
import functools
import math

import jax
import jax.numpy as jnp
from jax import lax
from jax.experimental import pallas as pl
from jax.experimental.pallas import tpu as pltpu

F32 = jnp.float32
BF16 = jnp.bfloat16

D_MODEL = 1024
PAGE_SIZE = 128
HA, DK = 4, 64
DV = 2 * DK
A_WIDTH = HA * DV
S5_GROUP, S5_STATE, S5_WIDTH = 16, 64, 512
S5_GROUPS = S5_WIDTH // S5_GROUP
S5_LANES = S5_GROUPS * S5_STATE
HC, NC = 8, 64
C_WIDTH = HC * NC
DECAY_LORA, AAA_LORA, GATE_LORA = 64, 64, 128
C_PROJ = 3 * C_WIDTH + DECAY_LORA + AAA_LORA + GATE_LORA
N_MEM, HX, DX = 256, 4, 64
X_WIDTH = HX * DX
RMS_EPS = 1e-6
GN_EPS = 64e-5
NEG_INF = -1e30

V7X_SUBLANES = 8
V7X_VMEM_BYTES = 64 * 1024 * 1024
VMEM_LIMIT = V7X_VMEM_BYTES - 8 * 1024 * 1024

ROW_TILE = 256
MERGE_TILE = 512
CROSS_TILE = 1024
NORM_TILE = 1024
ATT_TQ = 1024
ATT_TK = 1024
ATT_SUB = 256
ATT_ONES_ROWS = 16
PAGES_PER_STEP = 8
SAMPLE_SEQS_PER_STEP = 2
S5_TILE = 256
S5_LANE_CHUNK = 512
S5_MAP_LANES = 128
RW_CHUNK = 64
RW_CHUNKS_PER_STEP = 4


def _params(*sem):
    return pltpu.CompilerParams(dimension_semantics=sem, vmem_limit_bytes=VMEM_LIMIT)


def _resident(shape):
    nd = len(shape)
    return pl.BlockSpec(shape, lambda *_: (0,) * nd, pipeline_mode=pl.Buffered(1))


def _rms(x, g):
    return x * lax.rsqrt(jnp.mean(x * x, axis=-1, keepdims=True) + RMS_EPS) * g


def _dot(a, b):
    return jnp.dot(a, b, preferred_element_type=F32)


def _dot_nt(a, b):
    return lax.dot_general(a, b, (((1,), (1,)), ((), ())), preferred_element_type=F32)


def _dot_tn(a, b):
    return lax.dot_general(a, b, (((0,), (0,)), ((), ())), preferred_element_type=F32)


def _split3(x):
    hi = x.astype(BF16)
    r1 = x - hi.astype(F32)
    mid = r1.astype(BF16)
    lo = (r1 - mid.astype(F32)).astype(BF16)
    return hi, mid, lo


def _dot_exact_rhs(sel, x):
    hi, mid, lo = _split3(x)
    return _dot(sel, hi) + _dot(sel, mid) + _dot(sel, lo)


def _dot_exact_lhs(x, sel):
    hi, mid, lo = _split3(x)
    return _dot(hi, sel) + _dot(mid, sel) + _dot(lo, sel)


def _norm_proj_body(splits, t_splits, x_ref, g_ref, w_ref, *rest):
    hb = _rms(x_ref[...], g_ref[...]).astype(BF16)
    out_refs = rest[1:] if t_splits else rest
    k = 0
    for off, width, outs in splits:
        y = _dot(hb, w_ref[:, off:off + width])
        for fn, _ in outs:
            out_refs[k][...] = fn(y).astype(out_refs[k].dtype)
            k += 1
    for off, width, fn, _ in t_splits:
        out_refs[k][...] = fn(_dot_nt(rest[0][off:off + width, :], hb)).astype(out_refs[k].dtype)
        k += 1


def _norm_proj(x, g, w, splits, tm, wt=None, t_splits=()):
    n, d = x.shape
    out_shape, out_specs = [], []
    for _, width, outs in splits:
        for _, dt in outs:
            out_shape.append(jax.ShapeDtypeStruct((n, width), dt))
            out_specs.append(pl.BlockSpec((tm, width), lambda i: (i, 0)))
    for _, width, _, dt in t_splits:
        out_shape.append(jax.ShapeDtypeStruct((width, n), dt))
        out_specs.append(pl.BlockSpec((width, tm), lambda i: (0, i)))
    extra = [wt] if t_splits else []
    return pl.pallas_call(
        functools.partial(_norm_proj_body, splits, tuple(t_splits)),
        grid=(n // tm,),
        in_specs=[pl.BlockSpec((tm, d), lambda i: (i, 0)), _resident(g.shape), _resident(w.shape)]
                 + [_resident(e.shape) for e in extra],
        out_specs=out_specs,
        out_shape=out_shape,
        compiler_params=_params("parallel"),
        name="norm_proj",
    )(x, g, w, *extra)


def _mem_kv_body(x_ref, g_ref, wt_ref, o_ref):
    hb = _rms(x_ref[...], g_ref[...]).astype(BF16)
    o_ref[...] = _dot_nt(wt_ref[...], hb)


def _mem_kv(mem, g, wt):
    return pl.pallas_call(
        _mem_kv_body,
        out_shape=jax.ShapeDtypeStruct((wt.shape[0], mem.shape[0]), F32),
        compiler_params=pltpu.CompilerParams(vmem_limit_bytes=VMEM_LIMIT),
        name="mem_kv",
    )(mem, g, wt)


def _ident(y):
    return y


def _scale_q(y):
    return y * (DK ** -0.5)


def _scale_q_base2(y):
    return y * (DK ** -0.5 * math.log2(math.e))


def _attn_prompt_body(post_scale, qi_ref, kj_ref, lam_ref, qt_ref, k_ref, vt_ref, g_ref, o_ref,
                      m_sc, acc_sc):
    pair = pl.program_id(1)
    i = qi_ref[pair]
    j = kj_ref[pair]
    tq, tk = qt_ref.shape[1], k_ref.shape[0]
    ratio = tq // tk

    @pl.when(j == 0)
    def _():
        m_sc[...] = jnp.full(m_sc.shape, NEG_INF, F32)
        acc_sc[...] = jnp.zeros(acc_sc.shape, F32)

    def step(diagonal):
        qt = qt_ref[...]
        k = k_ref[...]
        vt = vt_ref[...]
        feat = lax.broadcasted_iota(jnp.int32, qt.shape, 0)
        zero = jnp.zeros_like(qt)
        qtc = [jnp.where(feat < DK, qt, zero), jnp.where(feat >= DK, qt, zero)]
        key_off = (j - i * ratio) * tk
        nsub = tk // ATT_SUB

        def first_query(u):
            return u * ATT_SUB if (diagonal and ratio == 1) else 0

        def scores(u, c):
            lo = first_query(u)
            s = _dot(k[u * ATT_SUB:(u + 1) * ATT_SUB], qtc[c][:, lo:])
            if diagonal:
                key = key_off + u * ATT_SUB + lax.broadcasted_iota(jnp.int32, s.shape, 0)
                qry = lo + lax.broadcasted_iota(jnp.int32, s.shape, 1)
                s = jnp.where(key <= qry, s, NEG_INF)
            return s

        st = [[scores(u, c) for c in range(2)] for u in range(nsub)]
        vt1 = jnp.concatenate([vt, jnp.ones((ATT_ONES_ROWS, tk), BF16)], axis=0)
        m_run = [m_sc[c] for c in range(2)]
        for u in range(nsub):
            lo = first_query(u)
            alpha, pb = [], []
            for c in range(2):
                m_prev = m_run[c][:, lo:]
                m_new = jnp.maximum(m_prev, jnp.max(st[u][c], axis=0, keepdims=True))
                alpha.append(jnp.exp2(m_prev - m_new))
                pb.append(jnp.exp2(st[u][c] - m_new).astype(BF16))
                m_run[c] = m_new if lo == 0 else jnp.concatenate([m_run[c][:, :lo], m_new], axis=1)
            for c in range(2):
                acc_sc[c, :, lo:] = (alpha[c] * acc_sc[c, :, lo:]
                                     + _dot(vt1[:, u * ATT_SUB:(u + 1) * ATT_SUB], pb[c]))
        for c in range(2):
            m_sc[c] = m_run[c]

    @pl.when(j < i * ratio)
    def _():
        step(False)

    @pl.when(j >= i * ratio)
    def _():
        step(True)

    @pl.when(j == (i + 1) * ratio - 1)
    def _():
        o0 = acc_sc[0, :DV, :] / acc_sc[0, DV:DV + 1, :]
        o1 = acc_sc[1, :DV, :] / acc_sc[1, DV:DV + 1, :]
        ot = o0 - lam_ref[0] * o1
        o_ref[...] = (_rms(ot.T, g_ref[...]) * post_scale).astype(o_ref.dtype)


def _attn_prompt(lam, qt, k, vt, subln, post_scale):
    n = k.shape[0]
    tk = min(ATT_TK, n)
    tq = min(ATT_TQ, n)
    ratio = tq // tk
    nq = n // tq
    qi = jnp.asarray([i for i in range(nq) for _ in range(ratio * (i + 1))], jnp.int32)
    kj = jnp.asarray([j for i in range(nq) for j in range(ratio * (i + 1))], jnp.int32)
    grid_spec = pltpu.PrefetchScalarGridSpec(
        num_scalar_prefetch=2,
        grid=(HA, qi.shape[0]),
        in_specs=[
            pl.BlockSpec(memory_space=pltpu.SMEM),
            pl.BlockSpec((DV, tq), lambda h, p, qi, kj: (h, qi[p])),
            pl.BlockSpec((tk, DV), lambda h, p, qi, kj: (kj[p], h)),
            pl.BlockSpec((DV, tk), lambda h, p, qi, kj: (h, kj[p])),
            pl.BlockSpec((1, DV), lambda h, p, qi, kj: (0, 0)),
        ],
        out_specs=pl.BlockSpec((tq, DV), lambda h, p, qi, kj: (qi[p], h)),
        scratch_shapes=[pltpu.VMEM((2, 1, tq), F32), pltpu.VMEM((2, DV + ATT_ONES_ROWS, tq), F32)],
    )
    return pl.pallas_call(
        functools.partial(_attn_prompt_body, post_scale),
        grid_spec=grid_spec,
        out_shape=jax.ShapeDtypeStruct((n, A_WIDTH), BF16),
        compiler_params=_params("parallel", "arbitrary"),
        name="attn_prompt",
    )(qi, kj, lam, qt, k, vt, subln)


def _attn_sample_body(post_scale, n_tok, pt_ref, lam_ref, q_ref, kn_ref, vn_ref, g_ref, *rest):
    del pt_ref
    npg = PAGES_PER_STEP
    nsq = SAMPLE_SEQS_PER_STEP
    k_refs = [rest[b * npg:(b + 1) * npg] for b in range(nsq)]
    v_refs = [rest[(nsq + b) * npg:(nsq + b + 1) * npg] for b in range(nsq)]
    o_ref, m_sc, l_sc, acc_sc = rest[2 * nsq * npg:]
    j = pl.program_id(1)
    hrows = 2 * n_tok
    nrow = HA * hrows
    width = HA * 2 * DK

    rid = lax.broadcasted_iota(jnp.int32, (nrow, width), 0)
    cid = lax.broadcasted_iota(jnp.int32, (nrow, width), 1)
    qblk = []
    for b in range(nsq):
        qrep = jnp.concatenate([q_ref[b]] * (2 * HA), axis=0)
        qblk.append(jnp.where(cid // DK == rid // n_tok, qrep, jnp.zeros_like(qrep)).astype(BF16))

    @pl.when(j == 0)
    def _():
        m_sc[...] = jnp.full(m_sc.shape, NEG_INF, F32)
        l_sc[...] = jnp.zeros(l_sc.shape, F32)
        acc_sc[...] = jnp.zeros(acc_sc.shape, F32)

    def update(s_pages, v_pages):
        seqs = range(nsq)
        m_prev = [m_sc[b] for b in seqs]
        s_max = []
        for b in seqs:
            s_red = s_pages[b][0]
            for s in s_pages[b][1:]:
                s_red = jnp.maximum(s_red, s)
            s_max.append(s_red)
        m_new = [jnp.maximum(m_prev[b], jnp.max(s_max[b], axis=-1, keepdims=True)) for b in seqs]
        alpha = [jnp.exp(m_prev[b] - m_new[b]) for b in seqs]
        p_sum = [None] * nsq
        pv = [[None] * HA for _ in seqs]
        for i in range(len(s_pages[0])):
            for b in seqs:
                p = jnp.exp(s_pages[b][i] - m_new[b])
                p_sum[b] = p if p_sum[b] is None else p_sum[b] + p
                pb = p.astype(BF16)
                for h in range(HA):
                    d = _dot(pb[h * hrows:(h + 1) * hrows], v_pages[b][i][h])
                    pv[b][h] = d if pv[b][h] is None else pv[b][h] + d
        for b in seqs:
            l_sc[b] = alpha[b] * l_sc[b] + jnp.sum(p_sum[b], axis=-1, keepdims=True)
            acc_sc[b] = alpha[b] * acc_sc[b] + jnp.concatenate(pv[b], axis=0)
            m_sc[b] = m_new[b]

    update([[_dot(qblk[b], k_refs[b][i][...].astype(BF16)) for i in range(npg)] for b in range(nsq)],
           [[[v_refs[b][i][pl.ds(h, PAGE_SIZE, stride=HA), :].astype(BF16) for h in range(HA)]
             for i in range(npg)] for b in range(nsq)])

    @pl.when(j == pl.num_programs(1) - 1)
    def _():
        pad = jnp.zeros((PAGE_SIZE - n_tok, width), F32)
        r2 = lax.broadcasted_iota(jnp.int32, (nrow, PAGE_SIZE), 0) % n_tok
        c2 = lax.broadcasted_iota(jnp.int32, (nrow, PAGE_SIZE), 1)
        s_new, v_new = [], []
        for b in range(nsq):
            kn = jnp.concatenate([kn_ref[b], pad], axis=0).astype(BF16)
            vn = jnp.concatenate([vn_ref[b], pad], axis=0).astype(BF16)
            s_new.append([jnp.where(c2 <= r2, _dot_nt(qblk[b], kn), NEG_INF)])
            v_new.append([[vn[:, h * DV:(h + 1) * DV] for h in range(HA)]])
        update(s_new, v_new)
        lam = lam_ref[0]
        for b in range(nsq):
            sel = acc_sc[b] / l_sc[b]
            heads = []
            for h in range(HA):
                o = sel[h * hrows:h * hrows + n_tok] - lam * sel[h * hrows + n_tok:(h + 1) * hrows]
                heads.append(_rms(o, g_ref[...]) * post_scale)
            o_ref[b] = jnp.concatenate(heads, axis=1).astype(o_ref.dtype)


def _attn_sample(layer, page_table, lam, q, k_new, v_new, subln, cache_kt, cache_vr, post_scale):
    b, n_tok, width = q.shape
    n_pages = page_table.shape[1]
    npg = PAGES_PER_STEP
    nsq = SAMPLE_SEQS_PER_STEP
    nj = n_pages // npg
    pt = page_table.reshape(-1)

    def page_spec(s, i):
        return pl.BlockSpec((None, None, width, PAGE_SIZE),
                            lambda bb, j, p: (layer, p[(bb * nsq + s) * n_pages + j * npg + i], 0, 0))

    pages = [page_spec(s, i) for s in range(nsq) for i in range(npg)]
    tok_spec = pl.BlockSpec((nsq, n_tok, width), lambda bb, j, p: (bb, 0, 0))
    nrow = 2 * HA * n_tok
    grid_spec = pltpu.PrefetchScalarGridSpec(
        num_scalar_prefetch=1,
        grid=(b // nsq, nj),
        in_specs=[pl.BlockSpec(memory_space=pltpu.SMEM), tok_spec, tok_spec, tok_spec,
                  pl.BlockSpec((1, DV), lambda bb, j, p: (0, 0))] + pages + pages,
        out_specs=tok_spec,
        scratch_shapes=[pltpu.VMEM((nsq, nrow, 1), F32), pltpu.VMEM((nsq, nrow, 1), F32),
                        pltpu.VMEM((nsq, nrow, DV), F32)],
    )
    n_ops = nsq * npg
    return pl.pallas_call(
        functools.partial(_attn_sample_body, post_scale, n_tok),
        grid_spec=grid_spec,
        out_shape=jax.ShapeDtypeStruct((b, n_tok, width), F32),
        compiler_params=_params("parallel", "arbitrary"),
        name="attn_sample",
    )(pt, lam, q, k_new, v_new, subln, *([cache_kt] * n_ops), *([cache_vr] * n_ops))


def _s5_body(chained, u_ref, x0r_ref, x0i_ref, tab_ref, bre_ref, bim_ref, cre_ref, cim_ref,
             d_ref, wg_ref, bg_ref, y_ref, xfr_ref, xfi_ref, sr_sc, si_sc, st_sc):
    i = pl.program_id(0)
    tt = u_ref.shape[0]
    sub = V7X_SUBLANES
    u = u_ref[...]
    ub = u.astype(BF16)
    nblk = bre_ref.shape[0]
    cw, sw = S5_WIDTH // nblk, S5_LANES // nblk
    for q in range(nblk):
        sr_sc[sub:, q * sw:(q + 1) * sw] = _dot(ub[:, q * cw:(q + 1) * cw], bre_ref[q])
        si_sc[sub:, q * sw:(q + 1) * sw] = _dot(ub[:, q * cw:(q + 1) * cw], bim_ref[q])

    if chained:
        @pl.when(i == 0)
        def _():
            st_sc[0:1, :] = x0r_ref[...]
            st_sc[1:2, :] = x0i_ref[...]

        sr_sc[sub - 1:sub, :] = st_sc[0:1, :]
        si_sc[sub - 1:sub, :] = st_sc[1:2, :]

    def group(gi, carry):
        r0 = pl.multiple_of(gi * sub, sub)
        for lc in range(S5_LANES // S5_LANE_CHUNK):
            ls = slice(lc * S5_LANE_CHUNK, (lc + 1) * S5_LANE_CHUNK)
            vr = sr_sc[pl.ds(r0 + sub, sub), ls]
            vi = si_sc[pl.ds(r0 + sub, sub), ls]
            for t, d in enumerate((1, 2, 4)):
                ar, ai = tab_ref[2 * t, :, ls], tab_ref[2 * t + 1, :, ls]
                pr, pi = pltpu.roll(vr, d, axis=0), pltpu.roll(vi, d, axis=0)
                vr, vi = vr + ar * pr - ai * pi, vi + ar * pi + ai * pr
            if chained:
                cr = sr_sc[pl.ds(r0 + sub - 1, 1), ls]
                ci = si_sc[pl.ds(r0 + sub - 1, 1), ls]
            else:
                cr = x0r_ref[pl.ds(gi, 1), ls]
                ci = x0i_ref[pl.ds(gi, 1), ls]
            cr = jnp.broadcast_to(cr, vr.shape)
            ci = jnp.broadcast_to(ci, vi.shape)
            pr, pi = tab_ref[6, :, ls], tab_ref[7, :, ls]
            vr, vi = vr + pr * cr - pi * ci, vi + pr * ci + pi * cr
            sr_sc[pl.ds(r0 + sub, sub), ls] = vr
            si_sc[pl.ds(r0 + sub, sub), ls] = vi
            if not chained:
                xfr_ref[pl.ds(gi, 1), ls] = vr[sub - 1:sub]
                xfi_ref[pl.ds(gi, 1), ls] = vi[sub - 1:sub]
        return carry

    lax.fori_loop(0, tt // sub, group, 0)

    if chained:
        st_sc[0:1, :] = sr_sc[tt + sub - 1:tt + sub, :]
        st_sc[1:2, :] = si_sc[tt + sub - 1:tt + sub, :]
        xfr_ref[...] = st_sc[0:1, :]
        xfi_ref[...] = st_sc[1:2, :]

    cx = []
    for q in range(nblk):
        xr = sr_sc[sub:, q * sw:(q + 1) * sw].astype(BF16)
        xi = si_sc[sub:, q * sw:(q + 1) * sw].astype(BF16)
        cx.append(_dot(xr, cre_ref[q]) - _dot(xi, cim_ref[q]))
    y = jnp.concatenate(cx, axis=1) + d_ref[...] * u
    z = y * (0.5 * (1.0 + jnp.tanh(math.sqrt(2.0 / math.pi) * (y + 0.044715 * (y * y * y)))))
    gate = jax.nn.sigmoid(_dot(z.astype(BF16), wg_ref[...]) + bg_ref[...])
    y_ref[...] = (z * gate).astype(y_ref.dtype)


def _s5(u, x0r, x0i, sp, chained):
    n = u.shape[0]
    tt = min(S5_TILE, n)
    ng = x0r.shape[0]
    if chained:
        st_spec = pl.BlockSpec((1, S5_LANES), lambda i: (0, 0))
    else:
        assert tt == n and ng * V7X_SUBLANES == n
        st_spec = pl.BlockSpec((ng, S5_LANES), lambda i: (0, 0))
    consts = [sp["tab"], sp["b_re"], sp["b_im"], sp["c_re"], sp["c_im"], sp["d"], sp["w_glu"], sp["b_glu"]]
    return pl.pallas_call(
        functools.partial(_s5_body, chained),
        grid=(n // tt,),
        in_specs=[pl.BlockSpec((tt, S5_WIDTH), lambda i: (i, 0)), st_spec, st_spec]
                 + [_resident(c.shape) for c in consts],
        out_specs=[pl.BlockSpec((tt, S5_WIDTH), lambda i: (i, 0)), st_spec, st_spec],
        out_shape=[jax.ShapeDtypeStruct((n, S5_WIDTH), BF16),
                   jax.ShapeDtypeStruct((ng, S5_LANES), F32),
                   jax.ShapeDtypeStruct((ng, S5_LANES), F32)],
        scratch_shapes=[pltpu.VMEM((tt + V7X_SUBLANES, S5_LANES), F32),
                        pltpu.VMEM((tt + V7X_SUBLANES, S5_LANES), F32),
                        pltpu.VMEM((2, S5_LANES), F32)],
        compiler_params=_params("arbitrary"),
        name="s5_branch",
    )(u, x0r, x0i, *consts)


def _rwkv_body(t_valid, pc_ref, prev_ref, sh_ref, s0_ref, mu_ref, w0_ref, w2_ref, a0_ref, a2_ref,
               g2_ref, kk_ref, ka_ref, rk_ref, lnw_ref, lnb_ref, ones_ref,
               y_ref, sf_ref, s_sc):
    j = pl.program_id(1)
    t = RW_CHUNK
    rows_in = pc_ref.shape[0]
    tp = max(rows_in, t)
    nch = tp // t

    @pl.when(j == 0)
    def _():
        s_sc[...] = s0_ref[...]

    pc = pc_ref[...]
    if rows_in < tp:
        pc = jnp.concatenate([pc, jnp.zeros((tp - rows_in, C_PROJ), F32)], axis=0)
    prev_row = jnp.where(j == 0, sh_ref[...], prev_ref[V7X_SUBLANES - 1:V7X_SUBLANES, :])
    row_id = lax.broadcasted_iota(jnp.int32, (tp, C_PROJ), 0)
    shifted = jnp.where(row_id == 0, jnp.broadcast_to(prev_row, pc.shape), pltpu.roll(pc, 1, axis=0))
    xm = pc + (shifted - pc) * mu_ref[...]

    c = C_WIDTH
    r = xm[:, 0:c]
    k = xm[:, c:2 * c]
    v = xm[:, 2 * c:3 * c]
    pw = xm[:, 3 * c:3 * c + DECAY_LORA]
    pa = xm[:, 3 * c + DECAY_LORA:3 * c + DECAY_LORA + AAA_LORA]
    pg = xm[:, 3 * c + DECAY_LORA + AAA_LORA:]

    wl = w0_ref[...] + _dot(jnp.tanh(pw).astype(BF16), w2_ref[...])
    neg = -wl
    softplus = jnp.maximum(neg, 0.0) + jnp.log1p(jnp.exp(-jnp.abs(neg)))
    logw = -jnp.exp(-softplus - 0.5)
    a = jax.nn.sigmoid(a0_ref[...] + _dot(pa.astype(BF16), a2_ref[...]))
    g = _dot(jax.nn.sigmoid(pg).astype(BF16), g2_ref[...])

    ones = ones_ref[...]
    kk = k * kk_ref[...]
    kk = kk / jnp.maximum(jnp.sqrt(_dot_exact_lhs(kk * kk, ones)), 1e-12)
    kh = k * (1.0 + (a - 1.0) * ka_ref[...])
    alpha = -kk
    beta = kk * a

    if t_valid < tp:
        valid = lax.broadcasted_iota(jnp.int32, (tp, c), 0) < t_valid
        zero = jnp.zeros((tp, c), F32)
        logw = jnp.where(valid, logw, zero)
        alpha = jnp.where(valid, alpha, zero)
        beta = jnp.where(valid, beta, zero)
        kh_m = jnp.where(valid, kh, zero)
        v_m = jnp.where(valid, v, zero)
    else:
        kh_m, v_m = kh, v

    ri = lax.broadcasted_iota(jnp.int32, (tp, tp), 0)
    ci = lax.broadcasted_iota(jnp.int32, (tp, tp), 1)
    blocktril = jnp.logical_and(ri // t == ci // t, ci <= ri).astype(BF16)
    cum = _dot_exact_rhs(blocktril, logw)
    c_end = jnp.concatenate(
        [jnp.broadcast_to(cum[(q + 1) * t - 1:(q + 1) * t, :], (t, c)) for q in range(nch)], axis=0)
    e_cum = jnp.exp(cum)
    e_neg = jnp.exp(-cum)
    e_tail = jnp.exp(c_end - cum)
    at = (alpha * jnp.exp(cum - logw)).astype(BF16)
    rt = r * e_cum
    rtb = rt.astype(BF16)
    bt = (beta * e_neg).astype(BF16)
    kt = (kh_m * e_neg).astype(BF16)
    bh = (beta * e_tail).astype(BF16)
    kx = (kh_m * e_tail).astype(BF16)
    vb = v_m.astype(BF16)
    w_end = jnp.exp(c_end)

    ti = lax.broadcasted_iota(jnp.int32, (t, t), 0)
    si = lax.broadcasted_iota(jnp.int32, (t, t), 1)
    incl = si <= ti
    strict = si < ti
    eye = (si == ti).astype(F32)

    pairs = [(q, h) for q in range(nch) for h in range(HC)]

    def blk(x, q, h):
        return x[q * t:(q + 1) * t, h * NC:(h + 1) * NC]

    def stage(fn):
        return [fn(q, h, i) for i, (q, h) in enumerate(pairs)]

    ar = stage(lambda q, h, i: jnp.concatenate([blk(at, q, h), blk(rtb, q, h)], axis=0))
    xb = stage(lambda q, h, i: _dot_nt(ar[i], blk(bt, q, h)))
    xk = stage(lambda q, h, i: _dot_nt(ar[i], blk(kt, q, h)))
    n_ab = stage(lambda q, h, i: jnp.where(strict, xb[i][:t], 0.0))
    a_br = stage(lambda q, h, i: jnp.where(incl, xb[i][t:], 0.0).astype(BF16))
    a_ak = stage(lambda q, h, i: jnp.where(strict, xk[i][:t], 0.0).astype(BF16))
    a_kr = stage(lambda q, h, i: jnp.where(incl, xk[i][t:], 0.0).astype(BF16))
    minv = stage(lambda q, h, i: eye + n_ab[i])
    pw2 = n_ab
    span = 1
    while span * 2 < t:
        pb = [x.astype(BF16) for x in pw2]
        pw2 = stage(lambda q, h, i: _dot(pb[i], pb[i]))
        minv = stage(lambda q, h, i: minv[i] + _dot(pw2[i].astype(BF16), minv[i].astype(BF16)))
        span *= 2
    minvb = [x.astype(BF16) for x in minv]
    akv = stage(lambda q, h, i: _dot(a_ak[i], blk(vb, q, h)).astype(BF16))
    p1 = stage(lambda q, h, i: _dot(minvb[i], blk(at, q, h)).astype(BF16))
    p2 = stage(lambda q, h, i: _dot(minvb[i], akv[i]).astype(BF16))
    qm = stage(lambda q, h, i: (blk(rt, q, h) + _dot(a_br[i], p1[i])).astype(BF16))
    zm = stage(lambda q, h, i: _dot(a_br[i], p2[i]) + _dot(a_kr[i], blk(vb, q, h)))
    gm = stage(lambda q, h, i: _dot_tn(p1[i], blk(bh, q, h)).astype(BF16))
    hm = stage(lambda q, h, i: _dot_tn(p2[i], blk(bh, q, h)) + _dot_tn(blk(vb, q, h), blk(kx, q, h)))

    s_cur = [s_sc[h] for h in range(HC)]
    y_rows = []
    for q in range(nch):
        ys = []
        for h in range(HC):
            i = q * HC + h
            sb = s_cur[h].astype(BF16)
            ys.append(_dot_nt(qm[i], sb) + zm[i])
            s_cur[h] = s_cur[h] * blk(w_end, q, h)[0:1, :] + _dot(sb, gm[i]) + hm[i]
        y_rows.append(jnp.concatenate(ys, axis=1))
    for h in range(HC):
        s_sc[h] = s_cur[h]
        sf_ref[h] = s_cur[h]

    y = y_rows[0] if nch == 1 else jnp.concatenate(y_rows, axis=0)
    inv_n = 1.0 / NC
    mean = _dot_exact_lhs(y, ones) * inv_n
    dlt = y - mean
    var = _dot_exact_lhs(dlt * dlt, ones) * inv_n
    yn = dlt * lax.rsqrt(var + GN_EPS) * lnw_ref[...] + lnb_ref[...]
    bonus = _dot_exact_lhs(r * kh * rk_ref[...], ones) * v
    out = (yn + bonus) * g
    y_ref[...] = out[:rows_in].astype(y_ref.dtype)


def _rwkv(pc, shift_prev, s0, rp, batch, seq):
    rows = min(RW_CHUNK * RW_CHUNKS_PER_STEP, seq)
    nt = seq // rows
    sub = V7X_SUBLANES
    per8 = seq // sub
    consts = [rp["mu"], rp["w0"], rp["w2"], rp["a0"], rp["a2"], rp["g2"], rp["k_k"], rp["k_a"],
              rp["r_k"], rp["ln_w"], rp["ln_b"], rp["ones"]]
    return pl.pallas_call(
        functools.partial(_rwkv_body, rows),
        grid=(batch, nt),
        in_specs=[
            pl.BlockSpec((rows, C_PROJ), lambda b, j: (b * nt + j, 0)),
            pl.BlockSpec((sub, C_PROJ), lambda b, j: (b * per8 + jnp.maximum(j * (rows // sub) - 1, 0), 0)),
            pl.BlockSpec((None, 1, C_PROJ), lambda b, j: (b, 0, 0)),
            pl.BlockSpec((None, HC, NC, NC), lambda b, j: (b, 0, 0, 0)),
        ] + [_resident(c.shape) for c in consts],
        out_specs=[pl.BlockSpec((rows, C_WIDTH), lambda b, j: (b * nt + j, 0)),
                   pl.BlockSpec((None, HC, NC, NC), lambda b, j: (b, 0, 0, 0))],
        out_shape=[jax.ShapeDtypeStruct((batch * seq, C_WIDTH), F32),
                   jax.ShapeDtypeStruct((batch, HC, NC, NC), F32)],
        scratch_shapes=[pltpu.VMEM((HC, NC, NC), F32)],
        compiler_params=_params("parallel", "arbitrary"),
        name="rwkv_branch",
    )(pc, pc, shift_prev, s0, *consts)


def _merge_body(x_ref, sg_ref, ya_ref, yb_ref, yc_ref, pa_ref, pb_ref, pc_ref, wo_ref, o_ref):
    d = D_MODEL
    sg = sg_ref[...]
    merged = (sg[:, 0:d] * _dot(ya_ref[...].astype(BF16), pa_ref[...])
              + sg[:, d:2 * d] * _dot(yb_ref[...].astype(BF16), pb_ref[...])
              + sg[:, 2 * d:3 * d] * _dot(yc_ref[...].astype(BF16), pc_ref[...]))
    o_ref[...] = x_ref[...] + _dot(merged.astype(BF16), wo_ref[...])


def _merge(x, sg, ya, yb, yc, wp, tm):
    n = x.shape[0]
    rows = lambda w: pl.BlockSpec((tm, w), lambda i: (i, 0))
    consts = [wp["proj_a"], wp["proj_b"], wp["proj_c"], wp["w_out"]]
    return pl.pallas_call(
        _merge_body,
        grid=(n // tm,),
        in_specs=[rows(D_MODEL), rows(3 * D_MODEL), rows(A_WIDTH), rows(S5_WIDTH), rows(C_WIDTH)]
                 + [_resident(c.shape) for c in consts],
        out_specs=rows(D_MODEL),
        out_shape=jax.ShapeDtypeStruct((n, D_MODEL), F32),
        compiler_params=_params("parallel"),
        name="merge_out",
    )(x, sg, ya, yb, yc, *consts)


def _cross_body(x_ref, g_ref, wq_ref, mk_ref, mv_ref, wo_ref, o_ref):
    x = x_ref[...]
    q = _dot(_rms(x, g_ref[...]).astype(BF16), wq_ref[...])
    mk = mk_ref[...].astype(BF16)
    mv = mv_ref[...].astype(BF16)
    heads = []
    for h in range(HX):
        hs = slice(h * DX, (h + 1) * DX)
        s = _dot(q[:, hs].astype(BF16), mk[hs, :]) * (DX ** -0.5)
        e = jnp.exp(s - jnp.max(s, axis=-1, keepdims=True))
        p = e / jnp.sum(e, axis=-1, keepdims=True)
        heads.append(_dot_nt(p.astype(BF16), mv[hs, :]))
    o = jnp.concatenate(heads, axis=1)
    o_ref[...] = x + _dot(o.astype(BF16), wo_ref[...])


def _cross(x, mem_k, mem_v, wp, batch, seq, tm):
    nt = seq // tm
    xs = pl.BlockSpec((tm, D_MODEL), lambda b, j: (b * nt + j, 0))
    ms = pl.BlockSpec((None, X_WIDTH, N_MEM), lambda b, j: (b, 0, 0))
    consts_a = [wp["norm_x"], wp["w_xq"]]
    return pl.pallas_call(
        _cross_body,
        grid=(batch, nt),
        in_specs=[xs] + [_resident(c.shape) for c in consts_a] + [ms, ms, _resident(wp["w_xo"].shape)],
        out_specs=xs,
        out_shape=jax.ShapeDtypeStruct((batch * seq, D_MODEL), F32),
        compiler_params=_params("parallel", "parallel"),
        name="cross_attn",
    )(x, *consts_a, mem_k, mem_v, wp["w_xo"])


def _ffn_body(x_ref, g_ref, wg_ref, wu_ref, wd_ref, o_ref):
    x = x_ref[...]
    hb = _rms(x, g_ref[...]).astype(BF16)
    gate = _dot(hb, wg_ref[...])
    up = _dot(hb, wu_ref[...])
    act = (gate * jax.nn.sigmoid(gate) * up).astype(BF16)
    o_ref[...] = x + _dot(act, wd_ref[...])


def _ffn(x, wp, tm):
    n = x.shape[0]
    rows = pl.BlockSpec((tm, D_MODEL), lambda i: (i, 0))
    consts = [wp["norm_ffn"], wp["w_gate"], wp["w_up"], wp["w_down"]]
    return pl.pallas_call(
        _ffn_body,
        grid=(n // tm,),
        in_specs=[rows] + [_resident(c.shape) for c in consts],
        out_specs=rows,
        out_shape=jax.ShapeDtypeStruct((n, D_MODEL), F32),
        compiler_params=_params("parallel"),
        name="swiglu",
    )(x, *consts)


def _final_norm_body(x_ref, g_ref, o_ref):
    o_ref[...] = _rms(x_ref[...], g_ref[...])


def _final_norm(x, g, tm):
    n = x.shape[0]
    rows = pl.BlockSpec((tm, D_MODEL), lambda i: (i, 0))
    return pl.pallas_call(
        _final_norm_body,
        grid=(n // tm,),
        in_specs=[rows, _resident(g.shape)],
        out_specs=rows,
        out_shape=jax.ShapeDtypeStruct((n, D_MODEL), F32),
        compiler_params=_params("parallel"),
        name="final_norm",
    )(x, g)


def _s5_params(lp):
    lam_re, lam_im = lp["s5_lam_re"], lp["s5_lam_im"]
    dt = jnp.exp(lp["s5_log_dt"])[:, None]
    mag = jnp.exp(lam_re * dt)
    ab_re = mag * jnp.cos(lam_im * dt)
    ab_im = mag * jnp.sin(lam_im * dt)
    den = lam_re * lam_re + lam_im * lam_im
    nr = ab_re - 1.0
    cr = (nr * lam_re + ab_im * lam_im) / den
    ci = (ab_im * lam_re - nr * lam_im) / den
    bb_re = cr[..., None] * lp["s5_b_re"] - ci[..., None] * lp["s5_b_im"]
    bb_im = cr[..., None] * lp["s5_b_im"] + ci[..., None] * lp["s5_b_re"]
    gpb = S5_MAP_LANES // S5_GROUP
    nblk = S5_GROUPS // gpb
    eye = jnp.eye(gpb, dtype=F32)

    def blockdiag_in(b):
        b = b.reshape(nblk, gpb, S5_STATE, S5_GROUP)
        return jnp.einsum("qgph,gk->qghkp", b, eye).reshape(nblk, gpb * S5_GROUP, gpb * S5_STATE)

    def blockdiag_out(cm):
        cm = cm.reshape(nblk, gpb, S5_GROUP, S5_STATE)
        return jnp.einsum("qghp,gk->qgpkh", cm, eye).reshape(nblk, gpb * S5_STATE, gpb * S5_GROUP)

    ar, ai = ab_re.reshape(-1), ab_im.reshape(-1)
    pows_r, pows_i = [ar], [ai]
    for _ in range(V7X_SUBLANES - 1):
        pr, pi = pows_r[-1], pows_i[-1]
        pows_r.append(pr * ar - pi * ai)
        pows_i.append(pr * ai + pi * ar)
    rows = jnp.arange(V7X_SUBLANES)[:, None]
    tabs = []
    for d in (1, 2, 4):
        tabs.append(jnp.where(rows >= d, pows_r[d - 1][None, :], 0.0))
        tabs.append(jnp.where(rows >= d, pows_i[d - 1][None, :], 0.0))
    tabs.append(jnp.stack(pows_r))
    tabs.append(jnp.stack(pows_i))
    return {
        "tab": jnp.stack(tabs).astype(F32),
        "b_re": blockdiag_in(bb_re).astype(BF16), "b_im": blockdiag_in(bb_im).astype(BF16),
        "c_re": blockdiag_out(lp["s5_c_re"]).astype(BF16), "c_im": blockdiag_out(lp["s5_c_im"]).astype(BF16),
        "d": lp["s5_d"][None, :], "w_glu": lp["s5_w_glu"].astype(BF16), "b_glu": lp["s5_b_glu"][None, :],
    }


def _rwkv_params(lp):
    seg = jnp.arange(C_WIDTH) // NC
    row = lambda a: a.reshape(1, -1)
    return {
        "mu": row(lp["rw_mu"]), "w0": row(lp["rw_w0"]), "w2": lp["rw_w2"].astype(BF16),
        "a0": row(lp["rw_a0"]), "a2": lp["rw_a2"].astype(BF16), "g2": lp["rw_g2"].astype(BF16),
        "k_k": row(lp["rw_k_k"]), "k_a": row(lp["rw_k_a"]), "r_k": row(lp["rw_r_k"]),
        "ln_w": row(lp["rw_ln_w"]), "ln_b": row(lp["rw_ln_b"]),
        "ones": (seg[:, None] == seg[None, :]).astype(BF16),
    }


def _in_proj(x, lp, tm, feature_major_qv):
    qa = HA * 2 * DK
    o_u = 3 * qa
    o_pc = o_u + S5_WIDTH
    o_g = o_pc + C_PROJ
    tail = [
        (o_u, S5_WIDTH, [(_ident, F32)]),
        (o_pc, C_PROJ, [(_ident, F32)]),
        (o_g, 3 * D_MODEL, [(jax.nn.sigmoid, F32)]),
    ]
    if feature_major_qv:
        splits = [(qa, qa, [(_ident, F32), (_ident, BF16)]), (2 * qa, qa, [(_ident, F32)])] + tail
        t_splits = [(0, qa, _scale_q_base2, BF16), (qa, qa, _ident, BF16)]
        k_f, k_b, v_f, u, pc, sg, q_t, v_t = _norm_proj(x, lp["norm_mix"], lp["w_in_b"], splits, tm,
                                                       wt=lp["w_qv_t"], t_splits=t_splits)
        return q_t, k_f, k_b, v_f, v_t, u, pc, sg
    splits = [(0, qa, [(_scale_q, F32)]), (qa, qa, [(_ident, F32)]), (2 * qa, qa, [(_ident, F32)])] + tail
    q, k_f, v_f, u, pc, sg = _norm_proj(x, lp["norm_mix"], lp["w_in_b"], splits, tm)
    return q, k_f, None, v_f, None, u, pc, sg


def _layer(l, x, batch, seq, attn_fn, mem_k, mem_v, ssm_re0, ssm_im0, rwkv0, shift0, lp):
    n = batch * seq
    tm = min(ROW_TILE, n)
    lam_init = 0.8 - 0.6 * math.exp(-0.3 * l)
    lam = (jnp.exp(jnp.sum(lp["lam_q1"] * lp["lam_k1"])) - jnp.exp(jnp.sum(lp["lam_q2"] * lp["lam_k2"]))
           + lam_init).reshape(1).astype(F32)
    q, k_f, k_b, v_f, v_t, u, pc, sg = _in_proj(x, lp, tm, feature_major_qv=(batch == 1))
    ya = attn_fn(lam, q, k_f, k_b, v_f, v_t, lp["subln"], 1.0 - lam_init)
    yb, ssm_re, ssm_im = _s5(u, ssm_re0, ssm_im0, lp["s5"], chained=(batch == 1))
    yc, rwkv_s = _rwkv(pc, shift0, rwkv0, lp["rw"], batch, seq)
    shift = pc.reshape(batch, seq, C_PROJ)[:, -1]
    x = _merge(x, sg, ya, yb, yc, lp, min(MERGE_TILE, n))
    x = _cross(x, mem_k, mem_v, lp, batch, seq, min(CROSS_TILE, seq))
    x = _ffn(x, lp, tm)
    return x, k_f, v_f, ssm_re, ssm_im, rwkv_s, shift


def kernel(x_prompt, x_sample, cache_k, cache_v, cache_mem_k, cache_mem_v, state_ssm_re, state_ssm_im, state_rwkv, state_shift, page_table, mem_prompt, norm_mix, w_in, lam_q1, lam_k1, lam_q2, lam_k2, subln, proj_a, s5_lam_re, s5_lam_im, s5_log_dt, s5_b_re, s5_b_im, s5_c_re, s5_c_im, s5_d, s5_w_glu, s5_b_glu, proj_b, rw_mu, rw_w0, rw_w2, rw_a0, rw_a2, rw_g2, rw_k_k, rw_k_a, rw_r_k, rw_ln_w, rw_ln_b, proj_c, w_out, norm_x, norm_mem, w_xq, w_mk, w_mv, w_xo, norm_ffn, w_gate, w_up, w_down, norm_final):
    depth = w_in.shape[0]
    bp, seq, d = x_prompt.shape
    bs, dseq, _ = x_sample.shape
    assert bp == 1
    pool = cache_k.shape[1]
    ck = cache_k.transpose(0, 1, 3, 4, 5, 2).reshape(depth, pool, HA * 2 * DK, PAGE_SIZE)
    cv = cache_v.reshape(depth, pool, PAGE_SIZE * HA, DV)

    xp = x_prompt.reshape(bp * seq, d)
    xs = x_sample.reshape(bs * dseq, d)
    zeros_ssm = jnp.zeros((bp, S5_LANES), F32)
    zeros_rwkv = jnp.zeros((bp, HC, NC, NC), F32)
    zeros_shift = jnp.zeros((bp, 1, C_PROJ), F32)
    outs = [[] for _ in range(14)]
    for l in range(depth):
        raw = dict(lam_q1=lam_q1[l], lam_k1=lam_k1[l], lam_q2=lam_q2[l], lam_k2=lam_k2[l],
                   s5_lam_re=s5_lam_re[l], s5_lam_im=s5_lam_im[l], s5_log_dt=s5_log_dt[l],
                   s5_b_re=s5_b_re[l], s5_b_im=s5_b_im[l], s5_c_re=s5_c_re[l], s5_c_im=s5_c_im[l],
                   s5_d=s5_d[l], s5_w_glu=s5_w_glu[l], s5_b_glu=s5_b_glu[l],
                   rw_mu=rw_mu[l], rw_w0=rw_w0[l], rw_w2=rw_w2[l], rw_a0=rw_a0[l], rw_a2=rw_a2[l],
                   rw_g2=rw_g2[l], rw_k_k=rw_k_k[l], rw_k_a=rw_k_a[l], rw_r_k=rw_r_k[l].reshape(-1),
                   rw_ln_w=rw_ln_w[l], rw_ln_b=rw_ln_b[l])
        lp = dict(raw)
        lp.update(
            norm_mix=norm_mix[l][None, :], w_in_b=w_in[l].astype(BF16), subln=subln[l][None, :],
            w_qv_t=jnp.concatenate([w_in[l][:, :HA * 2 * DK], w_in[l][:, 2 * HA * 2 * DK:3 * HA * 2 * DK]],
                                   axis=1).T.astype(BF16),
            proj_a=proj_a[l].astype(BF16), proj_b=proj_b[l].astype(BF16), proj_c=proj_c[l].astype(BF16),
            w_out=w_out[l].astype(BF16), norm_x=norm_x[l][None, :], w_xq=w_xq[l].astype(BF16),
            w_xo=w_xo[l].astype(BF16), norm_ffn=norm_ffn[l][None, :], w_gate=w_gate[l].astype(BF16),
            w_up=w_up[l].astype(BF16), w_down=w_down[l].astype(BF16),
            s5=_s5_params(raw), rw=_rwkv_params(raw))

        w_mkv_t = jnp.concatenate([w_mk[l], w_mv[l]], axis=1).T.astype(BF16)
        mkv_t = _mem_kv(mem_prompt.reshape(bp * N_MEM, d), norm_mem[l][None, :], w_mkv_t)
        mk_t = mkv_t[:X_WIDTH].reshape(bp, X_WIDTH, N_MEM)
        mv_t = mkv_t[X_WIDTH:].reshape(bp, X_WIDTH, N_MEM)

        def attn_p(lam, q_t, k_f, k_b, v_f, v_t, sub_g, post):
            return _attn_prompt(lam, q_t, k_b, v_t, sub_g, post)

        xp, kp, vp, srp, sip, rwp, shp = _layer(
            l, xp, bp, seq, attn_p, mk_t, mv_t, zeros_ssm, zeros_ssm, zeros_rwkv, zeros_shift, lp)

        def feature_major(mem):
            return mem.transpose(0, 2, 3, 1).reshape(mem.shape[0], X_WIDTH, N_MEM)

        def token_major(mem_t):
            return mem_t.reshape(bp, HX, DX, N_MEM).transpose(0, 3, 1, 2)

        def attn_s(lam, q, k_f, k_b, v_f, v_t, sub_g, post, _l=l):
            shp3 = (bs, dseq, HA * 2 * DK)
            return _attn_sample(_l, page_table, lam, q.reshape(shp3), k_f.reshape(shp3),
                                v_f.reshape(shp3), sub_g, ck, cv, post).reshape(bs * dseq, A_WIDTH)

        xs, ks, vs, srs, sis, rws, shs = _layer(
            l, xs, bs, dseq, attn_s, feature_major(cache_mem_k[l]), feature_major(cache_mem_v[l]),
            state_ssm_re[l].reshape(bs, S5_LANES), state_ssm_im[l].reshape(bs, S5_LANES),
            state_rwkv[l], state_shift[l].reshape(bs, 1, C_PROJ), lp)

        vals = [kp.reshape(bp, seq, HA, 2, DK), vp.reshape(bp, seq, HA, DV),
                token_major(mk_t), token_major(mv_t),
                srp.reshape(bp, S5_GROUPS, S5_STATE), sip.reshape(bp, S5_GROUPS, S5_STATE), rwp, shp,
                ks.reshape(bs, dseq, HA, 2, DK), vs.reshape(bs, dseq, HA, DV),
                srs.reshape(bs, S5_GROUPS, S5_STATE), sis.reshape(bs, S5_GROUPS, S5_STATE), rws, shs]
        for o, val in zip(outs, vals):
            o.append(val)

    y_prompt = _final_norm(xp, norm_final[None, :], min(NORM_TILE, bp * seq)).reshape(bp, seq, d)
    y_sample = _final_norm(xs, norm_final[None, :], min(NORM_TILE, bs * dseq)).reshape(bs, dseq, d)
    return (y_prompt, y_sample) + tuple(jnp.stack(o) for o in outs)
```

```python
import functools
import math

import jax
import jax.numpy as jnp
from jax import lax
from jax.experimental import pallas as pl
from jax.experimental.pallas import tpu as pltpu

F32 = jnp.float32
BF16 = jnp.bfloat16

D_MODEL = 1024
PAGE_SIZE = 128
HA, DK = 4, 64
DV = 2 * DK
A_WIDTH = HA * DV
S5_GROUP, S5_STATE, S5_WIDTH = 16, 64, 512
S5_GROUPS = S5_WIDTH // S5_GROUP
S5_LANES = S5_GROUPS * S5_STATE
HC, NC = 8, 64
C_WIDTH = HC * NC
DECAY_LORA, AAA_LORA, GATE_LORA = 64, 64, 128
C_PROJ = 3 * C_WIDTH + DECAY_LORA + AAA_LORA + GATE_LORA
N_MEM, HX, DX = 256, 4, 64
X_WIDTH = HX * DX
RMS_EPS = 1e-6
GN_EPS = 64e-5
NEG_INF = -1e30

V7X_SUBLANES = 8
V7X_VMEM_BYTES = 64 * 1024 * 1024
VMEM_LIMIT = V7X_VMEM_BYTES - 8 * 1024 * 1024

ROW_TILE = 256
MERGE_TILE = 512
CROSS_TILE = 1024
ATT_TQ = 1024
ATT_TK = 1024
ATT_SUB = 256
ATT_ONES_ROWS = 16
PAGES_PER_STEP = 8
SAMPLE_SEQS_PER_STEP = 2
S5_TILE = 256
S5_LANE_CHUNK = 512
S5_MAP_LANES = 128
RW_CHUNK = 64
RW_CHUNKS_PER_STEP = 4


def _params(*sem):
    return pltpu.CompilerParams(dimension_semantics=sem, vmem_limit_bytes=VMEM_LIMIT)


def _resident(shape):
    nd = len(shape)
    return pl.BlockSpec(shape, lambda *_: (0,) * nd, pipeline_mode=pl.Buffered(1))


def _rms(x, g):
    return x * lax.rsqrt(jnp.mean(x * x, axis=-1, keepdims=True) + RMS_EPS) * g


def _dot(a, b):
    return jnp.dot(a, b, preferred_element_type=F32)


def _dot_nt(a, b):
    return lax.dot_general(a, b, (((1,), (1,)), ((), ())), preferred_element_type=F32)


def _dot_tn(a, b):
    return lax.dot_general(a, b, (((0,), (0,)), ((), ())), preferred_element_type=F32)


def _split3(x):
    hi = x.astype(BF16)
    r1 = x - hi.astype(F32)
    mid = r1.astype(BF16)
    lo = (r1 - mid.astype(F32)).astype(BF16)
    return hi, mid, lo


def _dot_exact_rhs(sel, x):
    hi, mid, lo = _split3(x)
    return _dot(sel, hi) + _dot(sel, mid) + _dot(sel, lo)


def _dot_exact_lhs(x, sel):
    hi, mid, lo = _split3(x)
    return _dot(hi, sel) + _dot(mid, sel) + _dot(lo, sel)


def _norm_proj_body(splits, t_splits, x_ref, g_ref, w_ref, *rest):
    hb = _rms(x_ref[...], g_ref[...]).astype(BF16)
    out_refs = rest[1:] if t_splits else rest
    k = 0
    for off, width, outs in splits:
        y = _dot(hb, w_ref[:, off:off + width])
        for fn, _ in outs:
            out_refs[k][...] = fn(y).astype(out_refs[k].dtype)
            k += 1
    for off, width, fn, _ in t_splits:
        out_refs[k][...] = fn(_dot_nt(rest[0][off:off + width, :], hb)).astype(out_refs[k].dtype)
        k += 1


def _norm_proj(x, g, w, splits, tm, wt=None, t_splits=()):
    n, d = x.shape
    out_shape, out_specs = [], []
    for _, width, outs in splits:
        for _, dt in outs:
            out_shape.append(jax.ShapeDtypeStruct((n, width), dt))
            out_specs.append(pl.BlockSpec((tm, width), lambda i: (i, 0)))
    for _, width, _, dt in t_splits:
        out_shape.append(jax.ShapeDtypeStruct((width, n), dt))
        out_specs.append(pl.BlockSpec((width, tm), lambda i: (0, i)))
    extra = [wt] if t_splits else []
    return pl.pallas_call(
        functools.partial(_norm_proj_body, splits, tuple(t_splits)),
        grid=(n // tm,),
        in_specs=[pl.BlockSpec((tm, d), lambda i: (i, 0)), _resident(g.shape), _resident(w.shape)]
                 + [_resident(e.shape) for e in extra],
        out_specs=out_specs,
        out_shape=out_shape,
        compiler_params=_params("parallel"),
        name="norm_proj",
    )(x, g, w, *extra)


def _mem_kv_body(x_ref, g_ref, wt_ref, o_ref):
    hb = _rms(x_ref[...], g_ref[...]).astype(BF16)
    o_ref[...] = _dot_nt(wt_ref[...], hb)


def _mem_kv(mem, g, wt):
    return pl.pallas_call(
        _mem_kv_body,
        out_shape=jax.ShapeDtypeStruct((wt.shape[0], mem.shape[0]), F32),
        compiler_params=pltpu.CompilerParams(vmem_limit_bytes=VMEM_LIMIT),
        name="mem_kv",
    )(mem, g, wt)


def _ident(y):
    return y


def _scale_q(y):
    return y * (DK ** -0.5)


def _scale_q_base2(y):
    return y * (DK ** -0.5 * math.log2(math.e))


def _attn_prompt_body(post_scale, qi_ref, kj_ref, lam_ref, qt_ref, k_ref, vt_ref, g_ref, o_ref,
                      m_sc, acc_sc):
    pair = pl.program_id(1)
    i = qi_ref[pair]
    j = kj_ref[pair]
    tq, tk = qt_ref.shape[1], k_ref.shape[0]
    ratio = tq // tk

    @pl.when(j == 0)
    def _():
        m_sc[...] = jnp.full(m_sc.shape, NEG_INF, F32)
        acc_sc[...] = jnp.zeros(acc_sc.shape, F32)

    def step(diagonal):
        qt = qt_ref[...]
        k = k_ref[...]
        vt = vt_ref[...]
        feat = lax.broadcasted_iota(jnp.int32, qt.shape, 0)
        zero = jnp.zeros_like(qt)
        qtc = [jnp.where(feat < DK, qt, zero), jnp.where(feat >= DK, qt, zero)]
        key_off = (j - i * ratio) * tk
        nsub = tk // ATT_SUB

        def first_query(u):
            return u * ATT_SUB if (diagonal and ratio == 1) else 0

        def scores(u, c):
            lo = first_query(u)
            s = _dot(k[u * ATT_SUB:(u + 1) * ATT_SUB], qtc[c][:, lo:])
            if diagonal:
                key = key_off + u * ATT_SUB + lax.broadcasted_iota(jnp.int32, s.shape, 0)
                qry = lo + lax.broadcasted_iota(jnp.int32, s.shape, 1)
                s = jnp.where(key <= qry, s, NEG_INF)
            return s

        st = [[scores(u, c) for c in range(2)] for u in range(nsub)]
        vt1 = jnp.concatenate([vt, jnp.ones((ATT_ONES_ROWS, tk), BF16)], axis=0)
        m_run = [m_sc[c] for c in range(2)]
        for u in range(nsub):
            lo = first_query(u)
            alpha, pb = [], []
            for c in range(2):
                m_prev = m_run[c][:, lo:]
                m_new = jnp.maximum(m_prev, jnp.max(st[u][c], axis=0, keepdims=True))
                alpha.append(jnp.exp2(m_prev - m_new))
                pb.append(jnp.exp2(st[u][c] - m_new).astype(BF16))
                m_run[c] = m_new if lo == 0 else jnp.concatenate([m_run[c][:, :lo], m_new], axis=1)
            for c in range(2):
                acc_sc[c, :, lo:] = (alpha[c] * acc_sc[c, :, lo:]
                                     + _dot(vt1[:, u * ATT_SUB:(u + 1) * ATT_SUB], pb[c]))
        for c in range(2):
            m_sc[c] = m_run[c]

    @pl.when(j < i * ratio)
    def _():
        step(False)

    @pl.when(j >= i * ratio)
    def _():
        step(True)

    @pl.when(j == (i + 1) * ratio - 1)
    def _():
        o0 = acc_sc[0, :DV, :] / acc_sc[0, DV:DV + 1, :]
        o1 = acc_sc[1, :DV, :] / acc_sc[1, DV:DV + 1, :]
        ot = o0 - lam_ref[0] * o1
        o_ref[...] = (_rms(ot.T, g_ref[...]) * post_scale).astype(o_ref.dtype)


def _attn_prompt(lam, qt, k, vt, subln, post_scale):
    n = k.shape[0]
    tk = min(ATT_TK, n)
    tq = min(ATT_TQ, n)
    ratio = tq // tk
    nq = n // tq
    qi = jnp.asarray([i for i in range(nq) for _ in range(ratio * (i + 1))], jnp.int32)
    kj = jnp.asarray([j for i in range(nq) for j in range(ratio * (i + 1))], jnp.int32)
    grid_spec = pltpu.PrefetchScalarGridSpec(
        num_scalar_prefetch=2,
        grid=(HA, qi.shape[0]),
        in_specs=[
            pl.BlockSpec(memory_space=pltpu.SMEM),
            pl.BlockSpec((DV, tq), lambda h, p, qi, kj: (h, qi[p])),
            pl.BlockSpec((tk, DV), lambda h, p, qi, kj: (kj[p], h)),
            pl.BlockSpec((DV, tk), lambda h, p, qi, kj: (h, kj[p])),
            pl.BlockSpec((1, DV), lambda h, p, qi, kj: (0, 0)),
        ],
        out_specs=pl.BlockSpec((tq, DV), lambda h, p, qi, kj: (qi[p], h)),
        scratch_shapes=[pltpu.VMEM((2, 1, tq), F32), pltpu.VMEM((2, DV + ATT_ONES_ROWS, tq), F32)],
    )
    return pl.pallas_call(
        functools.partial(_attn_prompt_body, post_scale),
        grid_spec=grid_spec,
        out_shape=jax.ShapeDtypeStruct((n, A_WIDTH), BF16),
        compiler_params=_params("parallel", "arbitrary"),
        name="attn_prompt",
    )(qi, kj, lam, qt, k, vt, subln)


def _attn_sample_body(post_scale, n_tok, pt_ref, lam_ref, q_ref, kn_ref, vn_ref, g_ref, *rest):
    del pt_ref
    npg = PAGES_PER_STEP
    nsq = SAMPLE_SEQS_PER_STEP
    k_refs = [rest[b * npg:(b + 1) * npg] for b in range(nsq)]
    v_refs = [rest[(nsq + b) * npg:(nsq + b + 1) * npg] for b in range(nsq)]
    o_ref, m_sc, l_sc, acc_sc = rest[2 * nsq * npg:]
    j = pl.program_id(1)
    hrows = 2 * n_tok
    nrow = HA * hrows
    width = HA * 2 * DK

    rid = lax.broadcasted_iota(jnp.int32, (nrow, width), 0)
    cid = lax.broadcasted_iota(jnp.int32, (nrow, width), 1)
    qblk = []
    for b in range(nsq):
        qrep = jnp.concatenate([q_ref[b]] * (2 * HA), axis=0)
        qblk.append(jnp.where(cid // DK == rid // n_tok, qrep, jnp.zeros_like(qrep)).astype(BF16))

    @pl.when(j == 0)
    def _():
        m_sc[...] = jnp.full(m_sc.shape, NEG_INF, F32)
        l_sc[...] = jnp.zeros(l_sc.shape, F32)
        acc_sc[...] = jnp.zeros(acc_sc.shape, F32)

    def update(s_pages, v_pages):
        seqs = range(nsq)
        m_prev = [m_sc[b] for b in seqs]
        s_max = []
        for b in seqs:
            s_red = s_pages[b][0]
            for s in s_pages[b][1:]:
                s_red = jnp.maximum(s_red, s)
            s_max.append(s_red)
        m_new = [jnp.maximum(m_prev[b], jnp.max(s_max[b], axis=-1, keepdims=True)) for b in seqs]
        alpha = [jnp.exp(m_prev[b] - m_new[b]) for b in seqs]
        p_sum = [None] * nsq
        pv = [[None] * HA for _ in seqs]
        for i in range(len(s_pages[0])):
            for b in seqs:
                p = jnp.exp(s_pages[b][i] - m_new[b])
                p_sum[b] = p if p_sum[b] is None else p_sum[b] + p
                pb = p.astype(BF16)
                for h in range(HA):
                    d = _dot(pb[h * hrows:(h + 1) * hrows], v_pages[b][i][h])
                    pv[b][h] = d if pv[b][h] is None else pv[b][h] + d
        for b in seqs:
            l_sc[b] = alpha[b] * l_sc[b] + jnp.sum(p_sum[b], axis=-1, keepdims=True)
            acc_sc[b] = alpha[b] * acc_sc[b] + jnp.concatenate(pv[b], axis=0)
            m_sc[b] = m_new[b]

    update([[_dot(qblk[b], k_refs[b][i][...].astype(BF16)) for i in range(npg)] for b in range(nsq)],
           [[[v_refs[b][i][pl.ds(h, PAGE_SIZE, stride=HA), :].astype(BF16) for h in range(HA)]
             for i in range(npg)] for b in range(nsq)])

    @pl.when(j == pl.num_programs(1) - 1)
    def _():
        pad = jnp.zeros((PAGE_SIZE - n_tok, width), F32)
        r2 = lax.broadcasted_iota(jnp.int32, (nrow, PAGE_SIZE), 0) % n_tok
        c2 = lax.broadcasted_iota(jnp.int32, (nrow, PAGE_SIZE), 1)
        s_new, v_new = [], []
        for b in range(nsq):
            kn = jnp.concatenate([kn_ref[b], pad], axis=0).astype(BF16)
            vn = jnp.concatenate([vn_ref[b], pad], axis=0).astype(BF16)
            s_new.append([jnp.where(c2 <= r2, _dot_nt(qblk[b], kn), NEG_INF)])
            v_new.append([[vn[:, h * DV:(h + 1) * DV] for h in range(HA)]])
        update(s_new, v_new)
        lam = lam_ref[0]
        for b in range(nsq):
            sel = acc_sc[b] / l_sc[b]
            heads = []
            for h in range(HA):
                o = sel[h * hrows:h * hrows + n_tok] - lam * sel[h * hrows + n_tok:(h + 1) * hrows]
                heads.append(_rms(o, g_ref[...]) * post_scale)
            o_ref[b] = jnp.concatenate(heads, axis=1).astype(o_ref.dtype)


def _attn_sample(layer, page_table, lam, q, k_new, v_new, subln, cache_kt, cache_vr, post_scale):
    b, n_tok, width = q.shape
    n_pages = page_table.shape[1]
    npg = PAGES_PER_STEP
    nsq = SAMPLE_SEQS_PER_STEP
    nj = n_pages // npg
    pt = page_table.reshape(-1)

    def page_spec(s, i):
        return pl.BlockSpec((None, None, width, PAGE_SIZE),
                            lambda bb, j, p: (layer, p[(bb * nsq + s) * n_pages + j * npg + i], 0, 0))

    pages = [page_spec(s, i) for s in range(nsq) for i in range(npg)]
    tok_spec = pl.BlockSpec((nsq, n_tok, width), lambda bb, j, p: (bb, 0, 0))
    nrow = 2 * HA * n_tok
    grid_spec = pltpu.PrefetchScalarGridSpec(
        num_scalar_prefetch=1,
        grid=(b // nsq, nj),
        in_specs=[pl.BlockSpec(memory_space=pltpu.SMEM), tok_spec, tok_spec, tok_spec,
                  pl.BlockSpec((1, DV), lambda bb, j, p: (0, 0))] + pages + pages,
        out_specs=tok_spec,
        scratch_shapes=[pltpu.VMEM((nsq, nrow, 1), F32), pltpu.VMEM((nsq, nrow, 1), F32),
                        pltpu.VMEM((nsq, nrow, DV), F32)],
    )
    n_ops = nsq * npg
    return pl.pallas_call(
        functools.partial(_attn_sample_body, post_scale, n_tok),
        grid_spec=grid_spec,
        out_shape=jax.ShapeDtypeStruct((b, n_tok, width), F32),
        compiler_params=_params("parallel", "arbitrary"),
        name="attn_sample",
    )(pt, lam, q, k_new, v_new, subln, *([cache_kt] * n_ops), *([cache_vr] * n_ops))


def _s5_body(chained, u_ref, x0r_ref, x0i_ref, tab_ref, bre_ref, bim_ref, cre_ref, cim_ref,
             d_ref, wg_ref, bg_ref, y_ref, xfr_ref, xfi_ref, sr_sc, si_sc, st_sc):
    i = pl.program_id(0)
    tt = u_ref.shape[0]
    sub = V7X_SUBLANES
    u = u_ref[...]
    ub = u.astype(BF16)
    nblk = bre_ref.shape[0]
    cw, sw = S5_WIDTH // nblk, S5_LANES // nblk
    for q in range(nblk):
        sr_sc[sub:, q * sw:(q + 1) * sw] = _dot(ub[:, q * cw:(q + 1) * cw], bre_ref[q])
        si_sc[sub:, q * sw:(q + 1) * sw] = _dot(ub[:, q * cw:(q + 1) * cw], bim_ref[q])

    if chained:
        @pl.when(i == 0)
        def _():
            st_sc[0:1, :] = x0r_ref[...]
            st_sc[1:2, :] = x0i_ref[...]

        sr_sc[sub - 1:sub, :] = st_sc[0:1, :]
        si_sc[sub - 1:sub, :] = st_sc[1:2, :]

    def group(gi, carry):
        r0 = pl.multiple_of(gi * sub, sub)
        for lc in range(S5_LANES // S5_LANE_CHUNK):
            ls = slice(lc * S5_LANE_CHUNK, (lc + 1) * S5_LANE_CHUNK)
            vr = sr_sc[pl.ds(r0 + sub, sub), ls]
            vi = si_sc[pl.ds(r0 + sub, sub), ls]
            for t, d in enumerate((1, 2, 4)):
                ar, ai = tab_ref[2 * t, :, ls], tab_ref[2 * t + 1, :, ls]
                pr, pi = pltpu.roll(vr, d, axis=0), pltpu.roll(vi, d, axis=0)
                vr, vi = vr + ar * pr - ai * pi, vi + ar * pi + ai * pr
            if chained:
                cr = sr_sc[pl.ds(r0 + sub - 1, 1), ls]
                ci = si_sc[pl.ds(r0 + sub - 1, 1), ls]
            else:
                cr = x0r_ref[pl.ds(gi, 1), ls]
                ci = x0i_ref[pl.ds(gi, 1), ls]
            cr = jnp.broadcast_to(cr, vr.shape)
            ci = jnp.broadcast_to(ci, vi.shape)
            pr, pi = tab_ref[6, :, ls], tab_ref[7, :, ls]
            vr, vi = vr + pr * cr - pi * ci, vi + pr * ci + pi * cr
            sr_sc[pl.ds(r0 + sub, sub), ls] = vr
            si_sc[pl.ds(r0 + sub, sub), ls] = vi
            if not chained:
                xfr_ref[pl.ds(gi, 1), ls] = vr[sub - 1:sub]
                xfi_ref[pl.ds(gi, 1), ls] = vi[sub - 1:sub]
        return carry

    lax.fori_loop(0, tt // sub, group, 0)

    if chained:
        st_sc[0:1, :] = sr_sc[tt + sub - 1:tt + sub, :]
        st_sc[1:2, :] = si_sc[tt + sub - 1:tt + sub, :]
        xfr_ref[...] = st_sc[0:1, :]
        xfi_ref[...] = st_sc[1:2, :]

    cx = []
    for q in range(nblk):
        xr = sr_sc[sub:, q * sw:(q + 1) * sw].astype(BF16)
        xi = si_sc[sub:, q * sw:(q + 1) * sw].astype(BF16)
        cx.append(_dot(xr, cre_ref[q]) - _dot(xi, cim_ref[q]))
    y = jnp.concatenate(cx, axis=1) + d_ref[...] * u
    z = y * (0.5 * (1.0 + jnp.tanh(math.sqrt(2.0 / math.pi) * (y + 0.044715 * (y * y * y)))))
    gate = jax.nn.sigmoid(_dot(z.astype(BF16), wg_ref[...]) + bg_ref[...])
    y_ref[...] = (z * gate).astype(y_ref.dtype)


def _s5(u, x0r, x0i, sp, chained):
    n = u.shape[0]
    tt = min(S5_TILE, n)
    ng = x0r.shape[0]
    if chained:
        st_spec = pl.BlockSpec((1, S5_LANES), lambda i: (0, 0))
    else:
        assert tt == n and ng * V7X_SUBLANES == n
        st_spec = pl.BlockSpec((ng, S5_LANES), lambda i: (0, 0))
    consts = [sp["tab"], sp["b_re"], sp["b_im"], sp["c_re"], sp["c_im"], sp["d"], sp["w_glu"], sp["b_glu"]]
    return pl.pallas_call(
        functools.partial(_s5_body, chained),
        grid=(n // tt,),
        in_specs=[pl.BlockSpec((tt, S5_WIDTH), lambda i: (i, 0)), st_spec, st_spec]
                 + [_resident(c.shape) for c in consts],
        out_specs=[pl.BlockSpec((tt, S5_WIDTH), lambda i: (i, 0)), st_spec, st_spec],
        out_shape=[jax.ShapeDtypeStruct((n, S5_WIDTH), BF16),
                   jax.ShapeDtypeStruct((ng, S5_LANES), F32),
                   jax.ShapeDtypeStruct((ng, S5_LANES), F32)],
        scratch_shapes=[pltpu.VMEM((tt + V7X_SUBLANES, S5_LANES), F32),
                        pltpu.VMEM((tt + V7X_SUBLANES, S5_LANES), F32),
                        pltpu.VMEM((2, S5_LANES), F32)],
        compiler_params=_params("arbitrary"),
        name="s5_branch",
    )(u, x0r, x0i, *consts)


def _rwkv_body(t_valid, pc_ref, prev_ref, sh_ref, s0_ref, mu_ref, w0_ref, w2_ref, a0_ref, a2_ref,
               g2_ref, kk_ref, ka_ref, rk_ref, lnw_ref, lnb_ref, ones_ref,
               y_ref, sf_ref, s_sc):
    j = pl.program_id(1)
    t = RW_CHUNK
    rows_in = pc_ref.shape[0]
    tp = max(rows_in, t)
    nch = tp // t

    @pl.when(j == 0)
    def _():
        s_sc[...] = s0_ref[...]

    pc = pc_ref[...]
    if rows_in < tp:
        pc = jnp.concatenate([pc, jnp.zeros((tp - rows_in, C_PROJ), F32)], axis=0)
    prev_row = jnp.where(j == 0, sh_ref[...], prev_ref[V7X_SUBLANES - 1:V7X_SUBLANES, :])
    row_id = lax.broadcasted_iota(jnp.int32, (tp, C_PROJ), 0)
    shifted = jnp.where(row_id == 0, jnp.broadcast_to(prev_row, pc.shape), pltpu.roll(pc, 1, axis=0))
    xm = pc + (shifted - pc) * mu_ref[...]

    c = C_WIDTH
    r = xm[:, 0:c]
    k = xm[:, c:2 * c]
    v = xm[:, 2 * c:3 * c]
    pw = xm[:, 3 * c:3 * c + DECAY_LORA]
    pa = xm[:, 3 * c + DECAY_LORA:3 * c + DECAY_LORA + AAA_LORA]
    pg = xm[:, 3 * c + DECAY_LORA + AAA_LORA:]

    wl = w0_ref[...] + _dot(jnp.tanh(pw).astype(BF16), w2_ref[...])
    neg = -wl
    softplus = jnp.maximum(neg, 0.0) + jnp.log1p(jnp.exp(-jnp.abs(neg)))
    logw = -jnp.exp(-softplus - 0.5)
    a = jax.nn.sigmoid(a0_ref[...] + _dot(pa.astype(BF16), a2_ref[...]))
    g = _dot(jax.nn.sigmoid(pg).astype(BF16), g2_ref[...])

    ones = ones_ref[...]
    kk = k * kk_ref[...]
    kk = kk / jnp.maximum(jnp.sqrt(_dot_exact_lhs(kk * kk, ones)), 1e-12)
    kh = k * (1.0 + (a - 1.0) * ka_ref[...])
    alpha = -kk
    beta = kk * a

    if t_valid < tp:
        valid = lax.broadcasted_iota(jnp.int32, (tp, c), 0) < t_valid
        zero = jnp.zeros((tp, c), F32)
        logw = jnp.where(valid, logw, zero)
        alpha = jnp.where(valid, alpha, zero)
        beta = jnp.where(valid, beta, zero)
        kh_m = jnp.where(valid, kh, zero)
        v_m = jnp.where(valid, v, zero)
    else:
        kh_m, v_m = kh, v

    ri = lax.broadcasted_iota(jnp.int32, (tp, tp), 0)
    ci = lax.broadcasted_iota(jnp.int32, (tp, tp), 1)
    blocktril = jnp.logical_and(ri // t == ci // t, ci <= ri).astype(BF16)
    cum = _dot_exact_rhs(blocktril, logw)
    c_end = jnp.concatenate(
        [jnp.broadcast_to(cum[(q + 1) * t - 1:(q + 1) * t, :], (t, c)) for q in range(nch)], axis=0)
    e_cum = jnp.exp(cum)
    e_neg = jnp.exp(-cum)
    e_tail = jnp.exp(c_end - cum)
    at = (alpha * jnp.exp(cum - logw)).astype(BF16)
    rt = r * e_cum
    rtb = rt.astype(BF16)
    bt = (beta * e_neg).astype(BF16)
    kt = (kh_m * e_neg).astype(BF16)
    bh = (beta * e_tail).astype(BF16)
    kx = (kh_m * e_tail).astype(BF16)
    vb = v_m.astype(BF16)
    w_end = jnp.exp(c_end)

    ti = lax.broadcasted_iota(jnp.int32, (t, t), 0)
    si = lax.broadcasted_iota(jnp.int32, (t, t), 1)
    incl = si <= ti
    strict = si < ti
    eye = (si == ti).astype(F32)

    pairs = [(q, h) for q in range(nch) for h in range(HC)]

    def blk(x, q, h):
        return x[q * t:(q + 1) * t, h * NC:(h + 1) * NC]

    def stage(fn):
        return [fn(q, h, i) for i, (q, h) in enumerate(pairs)]

    ar = stage(lambda q, h, i: jnp.concatenate([blk(at, q, h), blk(rtb, q, h)], axis=0))
    xb = stage(lambda q, h, i: _dot_nt(ar[i], blk(bt, q, h)))
    xk = stage(lambda q, h, i: _dot_nt(ar[i], blk(kt, q, h)))
    n_ab = stage(lambda q, h, i: jnp.where(strict, xb[i][:t], 0.0))
    a_br = stage(lambda q, h, i: jnp.where(incl, xb[i][t:], 0.0).astype(BF16))
    a_ak = stage(lambda q, h, i: jnp.where(strict, xk[i][:t], 0.0).astype(BF16))
    a_kr = stage(lambda q, h, i: jnp.where(incl, xk[i][t:], 0.0).astype(BF16))
    minv = stage(lambda q, h, i: eye + n_ab[i])
    pw2 = n_ab
    span = 1
    while span * 2 < t:
        pb = [x.astype(BF16) for x in pw2]
        pw2 = stage(lambda q, h, i: _dot(pb[i], pb[i]))
        minv = stage(lambda q, h, i: minv[i] + _dot(pw2[i].astype(BF16), minv[i].astype(BF16)))
        span *= 2
    minvb = [x.astype(BF16) for x in minv]
    akv = stage(lambda q, h, i: _dot(a_ak[i], blk(vb, q, h)).astype(BF16))
    p1 = stage(lambda q, h, i: _dot(minvb[i], blk(at, q, h)).astype(BF16))
    p2 = stage(lambda q, h, i: _dot(minvb[i], akv[i]).astype(BF16))
    qm = stage(lambda q, h, i: (blk(rt, q, h) + _dot(a_br[i], p1[i])).astype(BF16))
    zm = stage(lambda q, h, i: _dot(a_br[i], p2[i]) + _dot(a_kr[i], blk(vb, q, h)))
    gm = stage(lambda q, h, i: _dot_tn(p1[i], blk(bh, q, h)).astype(BF16))
    hm = stage(lambda q, h, i: _dot_tn(p2[i], blk(bh, q, h)) + _dot_tn(blk(vb, q, h), blk(kx, q, h)))

    s_cur = [s_sc[h] for h in range(HC)]
    y_rows = []
    for q in range(nch):
        ys = []
        for h in range(HC):
            i = q * HC + h
            sb = s_cur[h].astype(BF16)
            ys.append(_dot_nt(qm[i], sb) + zm[i])
            s_cur[h] = s_cur[h] * blk(w_end, q, h)[0:1, :] + _dot(sb, gm[i]) + hm[i]
        y_rows.append(jnp.concatenate(ys, axis=1))
    for h in range(HC):
        s_sc[h] = s_cur[h]
        sf_ref[h] = s_cur[h]

    y = y_rows[0] if nch == 1 else jnp.concatenate(y_rows, axis=0)
    inv_n = 1.0 / NC
    mean = _dot_exact_lhs(y, ones) * inv_n
    dlt = y - mean
    var = _dot_exact_lhs(dlt * dlt, ones) * inv_n
    yn = dlt * lax.rsqrt(var + GN_EPS) * lnw_ref[...] + lnb_ref[...]
    bonus = _dot_exact_lhs(r * kh * rk_ref[...], ones) * v
    out = (yn + bonus) * g
    y_ref[...] = out[:rows_in].astype(y_ref.dtype)


def _rwkv(pc, shift_prev, s0, rp, batch, seq):
    rows = min(RW_CHUNK * RW_CHUNKS_PER_STEP, seq)
    nt = seq // rows
    sub = V7X_SUBLANES
    per8 = seq // sub
    consts = [rp["mu"], rp["w0"], rp["w2"], rp["a0"], rp["a2"], rp["g2"], rp["k_k"], rp["k_a"],
              rp["r_k"], rp["ln_w"], rp["ln_b"], rp["ones"]]
    return pl.pallas_call(
        functools.partial(_rwkv_body, rows),
        grid=(batch, nt),
        in_specs=[
            pl.BlockSpec((rows, C_PROJ), lambda b, j: (b * nt + j, 0)),
            pl.BlockSpec((sub, C_PROJ), lambda b, j: (b * per8 + jnp.maximum(j * (rows // sub) - 1, 0), 0)),
            pl.BlockSpec((None, 1, C_PROJ), lambda b, j: (b, 0, 0)),
            pl.BlockSpec((None, HC, NC, NC), lambda b, j: (b, 0, 0, 0)),
        ] + [_resident(c.shape) for c in consts],
        out_specs=[pl.BlockSpec((rows, C_WIDTH), lambda b, j: (b * nt + j, 0)),
                   pl.BlockSpec((None, HC, NC, NC), lambda b, j: (b, 0, 0, 0))],
        out_shape=[jax.ShapeDtypeStruct((batch * seq, C_WIDTH), F32),
                   jax.ShapeDtypeStruct((batch, HC, NC, NC), F32)],
        scratch_shapes=[pltpu.VMEM((HC, NC, NC), F32)],
        compiler_params=_params("parallel", "arbitrary"),
        name="rwkv_branch",
    )(pc, pc, shift_prev, s0, *consts)


def _merge_body(x_ref, sg_ref, ya_ref, yb_ref, yc_ref, pa_ref, pb_ref, pc_ref, wo_ref, o_ref):
    d = D_MODEL
    sg = sg_ref[...]
    merged = (sg[:, 0:d] * _dot(ya_ref[...].astype(BF16), pa_ref[...])
              + sg[:, d:2 * d] * _dot(yb_ref[...].astype(BF16), pb_ref[...])
              + sg[:, 2 * d:3 * d] * _dot(yc_ref[...].astype(BF16), pc_ref[...]))
    o_ref[...] = x_ref[...] + _dot(merged.astype(BF16), wo_ref[...])


def _merge(x, sg, ya, yb, yc, wp, tm):
    n = x.shape[0]
    rows = lambda w: pl.BlockSpec((tm, w), lambda i: (i, 0))
    consts = [wp["proj_a"], wp["proj_b"], wp["proj_c"], wp["w_out"]]
    return pl.pallas_call(
        _merge_body,
        grid=(n // tm,),
        in_specs=[rows(D_MODEL), rows(3 * D_MODEL), rows(A_WIDTH), rows(S5_WIDTH), rows(C_WIDTH)]
                 + [_resident(c.shape) for c in consts],
        out_specs=rows(D_MODEL),
        out_shape=jax.ShapeDtypeStruct((n, D_MODEL), F32),
        compiler_params=_params("parallel"),
        name="merge_out",
    )(x, sg, ya, yb, yc, *consts)


def _cross_body(x_ref, g_ref, wq_ref, mk_ref, mv_ref, wo_ref, o_ref):
    x = x_ref[...]
    q = _dot(_rms(x, g_ref[...]).astype(BF16), wq_ref[...])
    mk = mk_ref[...].astype(BF16)
    mv = mv_ref[...].astype(BF16)
    heads = []
    for h in range(HX):
        hs = slice(h * DX, (h + 1) * DX)
        s = _dot(q[:, hs].astype(BF16), mk[hs, :]) * (DX ** -0.5)
        e = jnp.exp(s - jnp.max(s, axis=-1, keepdims=True))
        p = e / jnp.sum(e, axis=-1, keepdims=True)
        heads.append(_dot_nt(p.astype(BF16), mv[hs, :]))
    o = jnp.concatenate(heads, axis=1)
    o_ref[...] = x + _dot(o.astype(BF16), wo_ref[...])


def _cross(x, mem_k, mem_v, wp, batch, seq, tm):
    nt = seq // tm
    xs = pl.BlockSpec((tm, D_MODEL), lambda b, j: (b * nt + j, 0))
    ms = pl.BlockSpec((None, X_WIDTH, N_MEM), lambda b, j: (b, 0, 0))
    consts_a = [wp["norm_x"], wp["w_xq"]]
    return pl.pallas_call(
        _cross_body,
        grid=(batch, nt),
        in_specs=[xs] + [_resident(c.shape) for c in consts_a] + [ms, ms, _resident(wp["w_xo"].shape)],
        out_specs=xs,
        out_shape=jax.ShapeDtypeStruct((batch * seq, D_MODEL), F32),
        compiler_params=_params("parallel", "parallel"),
        name="cross_attn",
    )(x, *consts_a, mem_k, mem_v, wp["w_xo"])


def _ffn_body(with_out_norm, x_ref, g_ref, wg_ref, wu_ref, wd_ref, *rest):
    x = x_ref[...]
    hb = _rms(x, g_ref[...]).astype(BF16)
    gate = _dot(hb, wg_ref[...])
    up = _dot(hb, wu_ref[...])
    act = (gate * jax.nn.sigmoid(gate) * up).astype(BF16)
    y = x + _dot(act, wd_ref[...])
    if with_out_norm:
        gf_ref, o_ref = rest
        o_ref[...] = _rms(y, gf_ref[...])
    else:
        rest[0][...] = y


def _ffn(x, wp, tm, out_norm_gain=None):
    n = x.shape[0]
    rows = pl.BlockSpec((tm, D_MODEL), lambda i: (i, 0))
    consts = [wp["norm_ffn"], wp["w_gate"], wp["w_up"], wp["w_down"]]
    if out_norm_gain is not None:
        consts.append(out_norm_gain)
    return pl.pallas_call(
        functools.partial(_ffn_body, out_norm_gain is not None),
        grid=(n // tm,),
        in_specs=[rows] + [_resident(c.shape) for c in consts],
        out_specs=rows,
        out_shape=jax.ShapeDtypeStruct((n, D_MODEL), F32),
        compiler_params=_params("parallel"),
        name="swiglu",
    )(x, *consts)


def _s5_params(lp):
    lam_re, lam_im = lp["s5_lam_re"], lp["s5_lam_im"]
    dt = jnp.exp(lp["s5_log_dt"])[:, None]
    mag = jnp.exp(lam_re * dt)
    ab_re = mag * jnp.cos(lam_im * dt)
    ab_im = mag * jnp.sin(lam_im * dt)
    den = lam_re * lam_re + lam_im * lam_im
    nr = ab_re - 1.0
    cr = (nr * lam_re + ab_im * lam_im) / den
    ci = (ab_im * lam_re - nr * lam_im) / den
    bb_re = cr[..., None] * lp["s5_b_re"] - ci[..., None] * lp["s5_b_im"]
    bb_im = cr[..., None] * lp["s5_b_im"] + ci[..., None] * lp["s5_b_re"]
    gpb = S5_MAP_LANES // S5_GROUP
    nblk = S5_GROUPS // gpb
    eye = jnp.eye(gpb, dtype=F32)

    def blockdiag_in(b):
        b = b.reshape(nblk, gpb, S5_STATE, S5_GROUP)
        return jnp.einsum("qgph,gk->qghkp", b, eye).reshape(nblk, gpb * S5_GROUP, gpb * S5_STATE)

    def blockdiag_out(cm):
        cm = cm.reshape(nblk, gpb, S5_GROUP, S5_STATE)
        return jnp.einsum("qghp,gk->qgpkh", cm, eye).reshape(nblk, gpb * S5_STATE, gpb * S5_GROUP)

    ar, ai = ab_re.reshape(-1), ab_im.reshape(-1)
    pows_r, pows_i = [ar], [ai]
    for _ in range(V7X_SUBLANES - 1):
        pr, pi = pows_r[-1], pows_i[-1]
        pows_r.append(pr * ar - pi * ai)
        pows_i.append(pr * ai + pi * ar)
    rows = jnp.arange(V7X_SUBLANES)[:, None]
    tabs = []
    for d in (1, 2, 4):
        tabs.append(jnp.where(rows >= d, pows_r[d - 1][None, :], 0.0))
        tabs.append(jnp.where(rows >= d, pows_i[d - 1][None, :], 0.0))
    tabs.append(jnp.stack(pows_r))
    tabs.append(jnp.stack(pows_i))
    return {
        "tab": jnp.stack(tabs).astype(F32),
        "b_re": blockdiag_in(bb_re).astype(BF16), "b_im": blockdiag_in(bb_im).astype(BF16),
        "c_re": blockdiag_out(lp["s5_c_re"]).astype(BF16), "c_im": blockdiag_out(lp["s5_c_im"]).astype(BF16),
        "d": lp["s5_d"][None, :], "w_glu": lp["s5_w_glu"].astype(BF16), "b_glu": lp["s5_b_glu"][None, :],
    }


def _rwkv_params(lp):
    seg = jnp.arange(C_WIDTH) // NC
    row = lambda a: a.reshape(1, -1)
    return {
        "mu": row(lp["rw_mu"]), "w0": row(lp["rw_w0"]), "w2": lp["rw_w2"].astype(BF16),
        "a0": row(lp["rw_a0"]), "a2": lp["rw_a2"].astype(BF16), "g2": lp["rw_g2"].astype(BF16),
        "k_k": row(lp["rw_k_k"]), "k_a": row(lp["rw_k_a"]), "r_k": row(lp["rw_r_k"]),
        "ln_w": row(lp["rw_ln_w"]), "ln_b": row(lp["rw_ln_b"]),
        "ones": (seg[:, None] == seg[None, :]).astype(BF16),
    }


def _in_proj(x, lp, tm, feature_major_qv):
    qa = HA * 2 * DK
    o_u = 3 * qa
    o_pc = o_u + S5_WIDTH
    o_g = o_pc + C_PROJ
    tail = [
        (o_u, S5_WIDTH, [(_ident, F32)]),
        (o_pc, C_PROJ, [(_ident, F32)]),
        (o_g, 3 * D_MODEL, [(jax.nn.sigmoid, F32)]),
    ]
    if feature_major_qv:
        splits = [(qa, qa, [(_ident, F32), (_ident, BF16)]), (2 * qa, qa, [(_ident, F32)])] + tail
        t_splits = [(0, qa, _scale_q_base2, BF16), (qa, qa, _ident, BF16)]
        k_f, k_b, v_f, u, pc, sg, q_t, v_t = _norm_proj(x, lp["norm_mix"], lp["w_in_b"], splits, tm,
                                                       wt=lp["w_qv_t"], t_splits=t_splits)
        return q_t, k_f, k_b, v_f, v_t, u, pc, sg
    splits = [(0, qa, [(_scale_q, F32)]), (qa, qa, [(_ident, F32)]), (2 * qa, qa, [(_ident, F32)])] + tail
    q, k_f, v_f, u, pc, sg = _norm_proj(x, lp["norm_mix"], lp["w_in_b"], splits, tm)
    return q, k_f, None, v_f, None, u, pc, sg


def _layer(l, x, batch, seq, attn_fn, mem_k, mem_v, ssm_re0, ssm_im0, rwkv0, shift0, lp):
    n = batch * seq
    tm = min(ROW_TILE, n)
    lam_init = 0.8 - 0.6 * math.exp(-0.3 * l)
    lam = (jnp.exp(jnp.sum(lp["lam_q1"] * lp["lam_k1"])) - jnp.exp(jnp.sum(lp["lam_q2"] * lp["lam_k2"]))
           + lam_init).reshape(1).astype(F32)
    q, k_f, k_b, v_f, v_t, u, pc, sg = _in_proj(x, lp, tm, feature_major_qv=(batch == 1))
    ya = attn_fn(lam, q, k_f, k_b, v_f, v_t, lp["subln"], 1.0 - lam_init)
    yb, ssm_re, ssm_im = _s5(u, ssm_re0, ssm_im0, lp["s5"], chained=(batch == 1))
    yc, rwkv_s = _rwkv(pc, shift0, rwkv0, lp["rw"], batch, seq)
    shift = pc.reshape(batch, seq, C_PROJ)[:, -1]
    x = _merge(x, sg, ya, yb, yc, lp, min(MERGE_TILE, n))
    x = _cross(x, mem_k, mem_v, lp, batch, seq, min(CROSS_TILE, seq))
    x = _ffn(x, lp, tm, out_norm_gain=lp.get("norm_out"))
    return x, k_f, v_f, ssm_re, ssm_im, rwkv_s, shift


def kernel(x_prompt, x_sample, cache_k, cache_v, cache_mem_k, cache_mem_v, state_ssm_re, state_ssm_im, state_rwkv, state_shift, page_table, mem_prompt, norm_mix, w_in, lam_q1, lam_k1, lam_q2, lam_k2, subln, proj_a, s5_lam_re, s5_lam_im, s5_log_dt, s5_b_re, s5_b_im, s5_c_re, s5_c_im, s5_d, s5_w_glu, s5_b_glu, proj_b, rw_mu, rw_w0, rw_w2, rw_a0, rw_a2, rw_g2, rw_k_k, rw_k_a, rw_r_k, rw_ln_w, rw_ln_b, proj_c, w_out, norm_x, norm_mem, w_xq, w_mk, w_mv, w_xo, norm_ffn, w_gate, w_up, w_down, norm_final):
    depth = w_in.shape[0]
    bp, seq, d = x_prompt.shape
    bs, dseq, _ = x_sample.shape
    assert bp == 1
    pool = cache_k.shape[1]
    ck = cache_k.transpose(0, 1, 3, 4, 5, 2).reshape(depth, pool, HA * 2 * DK, PAGE_SIZE)
    cv = cache_v.reshape(depth, pool, PAGE_SIZE * HA, DV)

    xp = x_prompt.reshape(bp * seq, d)
    xs = x_sample.reshape(bs * dseq, d)
    zeros_ssm = jnp.zeros((bp, S5_LANES), F32)
    zeros_rwkv = jnp.zeros((bp, HC, NC, NC), F32)
    zeros_shift = jnp.zeros((bp, 1, C_PROJ), F32)
    outs = [[] for _ in range(14)]
    for l in range(depth):
        raw = dict(lam_q1=lam_q1[l], lam_k1=lam_k1[l], lam_q2=lam_q2[l], lam_k2=lam_k2[l],
                   s5_lam_re=s5_lam_re[l], s5_lam_im=s5_lam_im[l], s5_log_dt=s5_log_dt[l],
                   s5_b_re=s5_b_re[l], s5_b_im=s5_b_im[l], s5_c_re=s5_c_re[l], s5_c_im=s5_c_im[l],
                   s5_d=s5_d[l], s5_w_glu=s5_w_glu[l], s5_b_glu=s5_b_glu[l],
                   rw_mu=rw_mu[l], rw_w0=rw_w0[l], rw_w2=rw_w2[l], rw_a0=rw_a0[l], rw_a2=rw_a2[l],
                   rw_g2=rw_g2[l], rw_k_k=rw_k_k[l], rw_k_a=rw_k_a[l], rw_r_k=rw_r_k[l].reshape(-1),
                   rw_ln_w=rw_ln_w[l], rw_ln_b=rw_ln_b[l])
        lp = dict(raw)
        lp.update(
            norm_mix=norm_mix[l][None, :], w_in_b=w_in[l].astype(BF16), subln=subln[l][None, :],
            w_qv_t=jnp.concatenate([w_in[l][:, :HA * 2 * DK], w_in[l][:, 2 * HA * 2 * DK:3 * HA * 2 * DK]],
                                   axis=1).T.astype(BF16),
            proj_a=proj_a[l].astype(BF16), proj_b=proj_b[l].astype(BF16), proj_c=proj_c[l].astype(BF16),
            w_out=w_out[l].astype(BF16), norm_x=norm_x[l][None, :], w_xq=w_xq[l].astype(BF16),
            w_xo=w_xo[l].astype(BF16), norm_ffn=norm_ffn[l][None, :], w_gate=w_gate[l].astype(BF16),
            w_up=w_up[l].astype(BF16), w_down=w_down[l].astype(BF16),
            s5=_s5_params(raw), rw=_rwkv_params(raw))
        if l == depth - 1:
            lp["norm_out"] = norm_final[None, :]

        w_mkv_t = jnp.concatenate([w_mk[l], w_mv[l]], axis=1).T.astype(BF16)
        mkv_t = _mem_kv(mem_prompt.reshape(bp * N_MEM, d), norm_mem[l][None, :], w_mkv_t)
        mk_t = mkv_t[:X_WIDTH].reshape(bp, X_WIDTH, N_MEM)
        mv_t = mkv_t[X_WIDTH:].reshape(bp, X_WIDTH, N_MEM)

        def attn_p(lam, q_t, k_f, k_b, v_f, v_t, sub_g, post):
            return _attn_prompt(lam, q_t, k_b, v_t, sub_g, post)

        xp, kp, vp, srp, sip, rwp, shp = _layer(
            l, xp, bp, seq, attn_p, mk_t, mv_t, zeros_ssm, zeros_ssm, zeros_rwkv, zeros_shift, lp)

        def feature_major(mem):
            return mem.transpose(0, 2, 3, 1).reshape(mem.shape[0], X_WIDTH, N_MEM)

        def token_major(mem_t):
            return mem_t.reshape(bp, HX, DX, N_MEM).transpose(0, 3, 1, 2)

        def attn_s(lam, q, k_f, k_b, v_f, v_t, sub_g, post, _l=l):
            shp3 = (bs, dseq, HA * 2 * DK)
            return _attn_sample(_l, page_table, lam, q.reshape(shp3), k_f.reshape(shp3),
                                v_f.reshape(shp3), sub_g, ck, cv, post).reshape(bs * dseq, A_WIDTH)

        xs, ks, vs, srs, sis, rws, shs = _layer(
            l, xs, bs, dseq, attn_s, feature_major(cache_mem_k[l]), feature_major(cache_mem_v[l]),
            state_ssm_re[l].reshape(bs, S5_LANES), state_ssm_im[l].reshape(bs, S5_LANES),
            state_rwkv[l], state_shift[l].reshape(bs, 1, C_PROJ), lp)

        vals = [kp.reshape(bp, seq, HA, 2, DK), vp.reshape(bp, seq, HA, DV),
                token_major(mk_t), token_major(mv_t),
                srp.reshape(bp, S5_GROUPS, S5_STATE), sip.reshape(bp, S5_GROUPS, S5_STATE), rwp, shp,
                ks.reshape(bs, dseq, HA, 2, DK), vs.reshape(bs, dseq, HA, DV),
                srs.reshape(bs, S5_GROUPS, S5_STATE), sis.reshape(bs, S5_GROUPS, S5_STATE), rws, shs]
        for o, val in zip(outs, vals):
            o.append(val)

    y_prompt = xp.reshape(bp, seq, d)
    y_sample = xs.reshape(bs, dseq, d)
    return (y_prompt, y_sample) + tuple(jnp.stack(o) for o in outs)
```
